```python
import math
import jax
import jax.numpy as jnp
from jax import lax
import numpy as np

D_MODEL = 1024
BATCH = 16
SEQ = 2048
DEPTH = 1
DEC_BATCH = 8
DEC_SEQ = 32
PAST_LEN = 2048

CHUNK = 64
N_PAST_CHUNKS = 8
WIN_ROWS = N_PAST_CHUNKS * CHUNK
BAND = (N_PAST_CHUNKS + 1) * CHUNK

D_ATTN = D_MODEL // 2
N_HEADS = 8
HEAD_DIM = D_ATTN // N_HEADS
ATTN_SCALE = HEAD_DIM ** -0.5
MAX_REL = 64
N_REL = 2 * MAX_REL + 1
D_SSM = D_MODEL // 2
SSM_GROUP = 16
N_SSM_GROUPS = D_SSM // SSM_GROUP
SSM_STATE = 64
D_MIX = D_ATTN + D_SSM
D_IN_PROJ = 3 * D_ATTN + D_SSM

N_EXPERTS = 32
TOP_K = 4
D_FF = D_MODEL
SWIGLU_LIMIT = 7.0
SWIGLU_ALPHA = 1.702

RMS_EPS = 1e-6
NEG_INF = -1e30

kernel_name = 'hymba_stream_s5_chunkattn_moe_step'


def _rmsnorm(x, g):
    xf = x.astype(jnp.float32)
    y = xf * lax.rsqrt(jnp.mean(xf * xf, axis=-1, keepdims=True) + RMS_EPS)
    return (y * g.astype(jnp.float32)).astype(x.dtype)


def _band_bias(q_pos, k_pos, rel_bias):
    q_chunk = q_pos // CHUNK
    k_chunk = k_pos // CHUNK
    allowed = ((k_pos[None, :] >= 0)
               & (k_chunk[None, :] <= q_chunk[:, None])
               & (k_chunk[None, :] >= q_chunk[:, None] - N_PAST_CHUNKS))
    rel = jnp.clip(k_pos[None, :] - q_pos[:, None], -MAX_REL, MAX_REL) + MAX_REL
    bias = jnp.transpose(rel_bias.astype(jnp.float32)[rel], (2, 0, 1))
    return jnp.where(allowed[None], bias, NEG_INF)


def _attend(q, k, v, bias):
    s = jnp.einsum('bqhd,bkhd->bhqk', q, k).astype(jnp.float32) * ATTN_SCALE + bias[None]
    p = jax.nn.softmax(s, axis=-1).astype(v.dtype)
    return jnp.einsum('bhqk,bkhd->bqhd', p, v)


def _chunk_band_attention_prompt(q, k, v, rel_bias):
    b, l, h, dh = q.shape
    n_chunks = l // CHUNK
    pad = ((0, 0), (WIN_ROWS, 0), (0, 0), (0, 0))
    k_pad = jnp.pad(k, pad)
    v_pad = jnp.pad(v, pad)

    def one_chunk(ci):
        start = ci * CHUNK
        q_c = lax.dynamic_slice_in_dim(q, start, CHUNK, axis=1)
        k_band = lax.dynamic_slice_in_dim(k_pad, start, BAND, axis=1)
        v_band = lax.dynamic_slice_in_dim(v_pad, start, BAND, axis=1)
        q_pos = start + jnp.arange(CHUNK, dtype=jnp.int32)
        k_pos = start - WIN_ROWS + jnp.arange(BAND, dtype=jnp.int32)
        return _attend(q_c, k_band, v_band, _band_bias(q_pos, k_pos, rel_bias))

    out = lax.map(one_chunk, jnp.arange(n_chunks, dtype=jnp.int32))
    return jnp.transpose(out, (1, 0, 2, 3, 4)).reshape(b, l, h * dh)


def _chunk_band_attention_sample(q, k, v, cache_k, cache_v, rel_bias):
    b, s, h, dh = q.shape
    n_cache = cache_k.shape[1]
    k_all = jnp.concatenate([cache_k.astype(k.dtype), k], axis=1)
    v_all = jnp.concatenate([cache_v.astype(v.dtype), v], axis=1)
    q_pos = PAST_LEN + jnp.arange(s, dtype=jnp.int32)
    k_pos = jnp.concatenate([PAST_LEN - n_cache + jnp.arange(n_cache, dtype=jnp.int32), q_pos])
    return _attend(q, k_all, v_all, _band_bias(q_pos, k_pos, rel_bias)).reshape(b, s, h * dh)


def _complex_scan_op(e1, e2):
    a1r, a1i, b1r, b1i = e1
    a2r, a2i, b2r, b2i = e2
    return (a2r * a1r - a2i * a1i,
            a2r * a1i + a2i * a1r,
            a2r * b1r - a2i * b1i + b2r,
            a2r * b1i + a2i * b1r + b2i)


def _s5_mixer(u, s0_re, s0_im, a_re, a_im, log_dt, b_re, b_im, c_re, c_im, d, w_glu, b_glu):
    bsz, l, _ = u.shape
    f32 = jnp.float32
    uf = u.astype(f32).reshape(bsz, l, N_SSM_GROUPS, SSM_GROUP)
    a_re = a_re.astype(f32)
    a_im = a_im.astype(f32)
    b_re = b_re.astype(f32)
    b_im = b_im.astype(f32)
    dt = jnp.exp(log_dt.astype(f32))[:, None]
    mag = jnp.exp(dt * a_re)
    abar_re = mag * jnp.cos(dt * a_im)
    abar_im = mag * jnp.sin(dt * a_im)
    den = a_re * a_re + a_im * a_im
    num_re = abar_re - 1.0
    coef_re = (num_re * a_re + abar_im * a_im) / den
    coef_im = (abar_im * a_re - num_re * a_im) / den
    bb_re = coef_re[..., None] * b_re - coef_im[..., None] * b_im
    bb_im = coef_re[..., None] * b_im + coef_im[..., None] * b_re
    bu_re = jnp.einsum('blgi,gpi->lbgp', uf, bb_re)
    bu_im = jnp.einsum('blgi,gpi->lbgp', uf, bb_im)
    shape_a = (l, 1, N_SSM_GROUPS, SSM_STATE)
    elems = (jnp.broadcast_to(abar_re, shape_a), jnp.broadcast_to(abar_im, shape_a), bu_re, bu_im)
    _, _, x_re, x_im = lax.associative_scan(_complex_scan_op, elems, axis=0)
    if s0_re is not None:
        steps = jnp.arange(1, l + 1, dtype=f32)[:, None, None, None]
        p_mag = jnp.exp(steps * dt * a_re)
        p_ang = steps * dt * a_im
        p_re = p_mag * jnp.cos(p_ang)
        p_im = p_mag * jnp.sin(p_ang)
        s_re = s0_re.astype(f32)[None]
        s_im = s0_im.astype(f32)[None]
        x_re = x_re + p_re * s_re - p_im * s_im
        x_im = x_im + p_re * s_im + p_im * s_re
    y = (jnp.einsum('lbgp,gip->blgi', x_re, c_re.astype(f32))
         - jnp.einsum('lbgp,gip->blgi', x_im, c_im.astype(f32))
         + d.astype(f32) * uf)
    y = jax.nn.gelu(y.reshape(bsz, l, D_SSM))
    out = y * jax.nn.sigmoid(y @ w_glu.astype(f32) + b_glu.astype(f32))
    return out.astype(u.dtype), x_re[-1], x_im[-1]


def _split_proj(h, w_in):
    b, l, _ = h.shape
    proj = h @ w_in
    q = proj[..., :D_ATTN].reshape(b, l, N_HEADS, HEAD_DIM)
    k = proj[..., D_ATTN:2 * D_ATTN].reshape(b, l, N_HEADS, HEAD_DIM)
    v = proj[..., 2 * D_ATTN:3 * D_ATTN].reshape(b, l, N_HEADS, HEAD_DIM)
    u = proj[..., 3 * D_ATTN:]
    return q, k, v, u


def _mixer_prompt(h, w_in, w_out, rel_bias, ssm_p):
    l = h.shape[1]
    q, k, v, u = _split_proj(h, w_in)
    attn = _chunk_band_attention_prompt(q, k, v, rel_bias)
    ssm, s_re, s_im = _s5_mixer(u, None, None, *ssm_p)
    out = jnp.concatenate([attn, ssm.astype(attn.dtype)], axis=-1) @ w_out
    keep = min(WIN_ROWS, l)
    return out, (k[:, l - keep:], v[:, l - keep:], s_re, s_im)


def _mixer_sample(h, cache_k, cache_v, s0_re, s0_im, w_in, w_out, rel_bias, ssm_p):
    q, k, v, u = _split_proj(h, w_in)
    attn = _chunk_band_attention_sample(q, k, v, cache_k, cache_v, rel_bias)
    ssm, s_re, s_im = _s5_mixer(u, s0_re, s0_im, *ssm_p)
    out = jnp.concatenate([attn, ssm.astype(attn.dtype)], axis=-1) @ w_out
    return out, (k, v, s_re, s_im)


def _moe_ffn(h, w_router, b_router, w_gate_up, b_gate_up, w_down, b_down):
    b, l, d = h.shape
    t = h.reshape(b * l, d)
    logits = (t @ w_router + b_router).astype(jnp.float32)
    top_v, top_i = lax.top_k(logits, TOP_K)
    top_w = jax.nn.softmax(top_v, axis=-1)
    gates = jnp.einsum('tk,tke->te', top_w, jax.nn.one_hot(top_i, N_EXPERTS, dtype=jnp.float32))
    out = jnp.zeros((b * l, d), jnp.float32)
    for e in range(N_EXPERTS):
        gu = t @ w_gate_up[e] + b_gate_up[e]
        glu = jnp.minimum(gu[:, 0::2], SWIGLU_LIMIT)
        lin = jnp.clip(gu[:, 1::2], -SWIGLU_LIMIT, SWIGLU_LIMIT)
        act = glu * jax.nn.sigmoid(SWIGLU_ALPHA * glu) * (lin + 1.0)
        out = out + gates[:, e:e + 1] * (act @ w_down[e] + b_down[e])
    return out.reshape(b, l, d).astype(h.dtype)


def _layer(x, c, mixer_fn, norm_p, moe_p):
    w_ada, b_ada, g_pre_mix, g_post_mix, g_pre_ffn, g_post_ffn = norm_p
    mod = (jax.nn.silu(c) @ w_ada + b_ada)[:, None, :]
    sh1, sc1, gt1, sh2, sc2, gt2 = jnp.split(mod, 6, axis=-1)
    h = _rmsnorm(x, g_pre_mix) * (1.0 + sc1) + sh1
    m, state = mixer_fn(h)
    x = x + gt1 * _rmsnorm(m, g_post_mix)
    h = _rmsnorm(x, g_pre_ffn) * (1.0 + sc2) + sh2
    x = x + gt2 * _rmsnorm(_moe_ffn(h, *moe_p), g_post_ffn)
    return x, state


def setup_inputs(seed: int = 0) -> dict:
    key = jax.random.key(seed)
    ks = jax.random.split(key, 34)
    f32 = jnp.float32

    def nrm(k, shape, scale):
        return scale * jax.random.normal(k, shape, f32)

    n_cache = min(WIN_ROWS, PAST_LEN)
    g, p = N_SSM_GROUPS, SSM_STATE
    a_im0 = math.pi * jnp.arange(p, dtype=f32)
    return {
        'x_prompt': nrm(ks[0], (BATCH, SEQ, D_MODEL), 1.0),
        'x_sample': nrm(ks[1], (DEC_BATCH, DEC_SEQ, D_MODEL), 1.0),
        'cache_k': nrm(ks[2], (DEPTH, DEC_BATCH, n_cache, N_HEADS, HEAD_DIM), 1.0),
        'cache_v': nrm(ks[3], (DEPTH, DEC_BATCH, n_cache, N_HEADS, HEAD_DIM), 1.0),
        'state_ssm_re': nrm(ks[4], (DEPTH, DEC_BATCH, g, p), 0.1),
        'state_ssm_im': nrm(ks[5], (DEPTH, DEC_BATCH, g, p), 0.1),
        'c_prompt': nrm(ks[6], (BATCH, D_MODEL), 1.0),
        'c_sample': nrm(ks[7], (DEC_BATCH, D_MODEL), 1.0),
        'w_ada': nrm(ks[8], (DEPTH, D_MODEL, 6 * D_MODEL), 0.5 * D_MODEL ** -0.5),
        'b_ada': nrm(ks[9], (DEPTH, 6 * D_MODEL), 0.01),
        'g_pre_mix': 1.0 + nrm(ks[10], (DEPTH, D_MODEL), 0.01),
        'g_post_mix': 1.0 + nrm(ks[11], (DEPTH, D_MODEL), 0.01),
        'g_pre_ffn': 1.0 + nrm(ks[12], (DEPTH, D_MODEL), 0.01),
        'g_post_ffn': 1.0 + nrm(ks[13], (DEPTH, D_MODEL), 0.01),
        'w_in': nrm(ks[14], (DEPTH, D_MODEL, D_IN_PROJ), D_MODEL ** -0.5),
        'rel_bias': nrm(ks[15], (DEPTH, N_REL, N_HEADS), 0.1),
        'ssm_a_re': -0.5 + nrm(ks[16], (DEPTH, g, p), 0.01),
        'ssm_a_im': a_im0 + nrm(ks[17], (DEPTH, g, p), 0.01),
        'ssm_log_dt': jax.random.uniform(ks[18], (DEPTH, g), f32, math.log(1e-3), math.log(1e-1)),
        'ssm_b_re': nrm(ks[19], (DEPTH, g, p, SSM_GROUP), (2 * SSM_GROUP) ** -0.5),
        'ssm_b_im': nrm(ks[20], (DEPTH, g, p, SSM_GROUP), (2 * SSM_GROUP) ** -0.5),
        'ssm_c_re': nrm(ks[21], (DEPTH, g, SSM_GROUP, p), p ** -0.5),
        'ssm_c_im': nrm(ks[22], (DEPTH, g, SSM_GROUP, p), p ** -0.5),
        'ssm_d': nrm(ks[23], (DEPTH, g, SSM_GROUP), 1.0),
        'w_glu': nrm(ks[24], (DEPTH, D_SSM, D_SSM), D_SSM ** -0.5),
        'b_glu': nrm(ks[25], (DEPTH, D_SSM), 0.01),
        'w_out': nrm(ks[26], (DEPTH, D_MIX, D_MODEL), D_MIX ** -0.5),
        'w_router': nrm(ks[27], (DEPTH, D_MODEL, N_EXPERTS), D_MODEL ** -0.5),
        'b_router': nrm(ks[28], (DEPTH, N_EXPERTS), 0.01),
        'w_gate_up': nrm(ks[29], (DEPTH, N_EXPERTS, D_MODEL, 2 * D_FF), D_MODEL ** -0.5),
        'b_gate_up': nrm(ks[30], (DEPTH, N_EXPERTS, 2 * D_FF), 0.01),
        'w_down': nrm(ks[31], (DEPTH, N_EXPERTS, D_FF, D_MODEL), D_FF ** -0.5),
        'b_down': nrm(ks[32], (DEPTH, N_EXPERTS, D_MODEL), 0.01),
    }


def reference(x_prompt, x_sample, cache_k, cache_v, state_ssm_re, state_ssm_im, c_prompt, c_sample,
              w_ada, b_ada, g_pre_mix, g_post_mix, g_pre_ffn, g_post_ffn, w_in, rel_bias,
              ssm_a_re, ssm_a_im, ssm_log_dt, ssm_b_re, ssm_b_im, ssm_c_re, ssm_c_im, ssm_d,
              w_glu, b_glu, w_out, w_router, b_router, w_gate_up, b_gate_up, w_down, b_down):
    y_p = x_prompt
    y_s = x_sample
    kp_l, vp_l, sre_p_l, sim_p_l = [], [], [], []
    ks_l, vs_l, sre_s_l, sim_s_l = [], [], [], []
    for l in range(DEPTH):
        ssm_p = (ssm_a_re[l], ssm_a_im[l], ssm_log_dt[l], ssm_b_re[l], ssm_b_im[l],
                 ssm_c_re[l], ssm_c_im[l], ssm_d[l], w_glu[l], b_glu[l])
        moe_p = (w_router[l], b_router[l], w_gate_up[l], b_gate_up[l], w_down[l], b_down[l])
        norm_p = (w_ada[l], b_ada[l], g_pre_mix[l], g_post_mix[l], g_pre_ffn[l], g_post_ffn[l])
        y_p, (k_p, v_p, sre_p, sim_p) = _layer(
            y_p, c_prompt,
            lambda h: _mixer_prompt(h, w_in[l], w_out[l], rel_bias[l], ssm_p),
            norm_p, moe_p)
        y_s, (k_s, v_s, sre_s, sim_s) = _layer(
            y_s, c_sample,
            lambda h: _mixer_sample(h, cache_k[l], cache_v[l], state_ssm_re[l], state_ssm_im[l],
                                    w_in[l], w_out[l], rel_bias[l], ssm_p),
            norm_p, moe_p)
        kp_l.append(k_p)
        vp_l.append(v_p)
        sre_p_l.append(sre_p)
        sim_p_l.append(sim_p)
        ks_l.append(k_s)
        vs_l.append(v_s)
        sre_s_l.append(sre_s)
        sim_s_l.append(sim_s)
    return (y_p, y_s,
            jnp.stack(kp_l), jnp.stack(vp_l), jnp.stack(sre_p_l), jnp.stack(sim_p_l),
            jnp.stack(ks_l), jnp.stack(vs_l), jnp.stack(sre_s_l), jnp.stack(sim_s_l))
```

```python
import functools
import math

import jax
import jax.numpy as jnp
from jax import lax
from jax.experimental import pallas as pl
from jax.experimental.pallas import tpu as pltpu

F32 = jnp.float32
BF16 = jnp.bfloat16
I32 = jnp.int32
HIGHEST = lax.Precision.HIGHEST

CHUNK = 64
CHUNK_LOG2 = 6
N_PAST_CHUNKS = 8
PAST_LEN = 2048
WIN_ROWS = N_PAST_CHUNKS * CHUNK
MAX_REL = 64
N_REL = 2 * MAX_REL + 1
N_HEADS = 8
HEAD_DIM = 64
D_ATTN = N_HEADS * HEAD_DIM
SSM_GROUP = 16
SSM_STATE = 64
N_EXPERTS = 32
TOP_K = 4
SWIGLU_LIMIT = 7.0
SWIGLU_ALPHA = 1.702
RMS_EPS = 1e-6
NEG_INF = -1e30

LANES = 128
BF16_ROWS = 16
VMEM_LIMIT = 56 * 1024 * 1024

ROW_TILE = 512
Q_TILE = 256
SSM_PACK = 8
SSM_TIME = 64
FFN_TILE = 512


def _cparams(sem, vmem=None):
    return pltpu.CompilerParams(dimension_semantics=sem, vmem_limit_bytes=vmem)


def _rms(x, g):
    return x * lax.rsqrt(jnp.mean(x * x, axis=-1, keepdims=True) + RMS_EPS) * g


def _ada_kernel(c_ref, w_ref, b_ref, o_ref):
    c = c_ref[...]
    s = c * jax.nn.sigmoid(c)
    o_ref[...] = jnp.dot(s, w_ref[...], precision=HIGHEST, preferred_element_type=F32) + b_ref[...]


def _ada(c, w_ada, b_ada):
    n, d = c.shape
    nout = w_ada.shape[1]
    return pl.pallas_call(
        _ada_kernel,
        grid=(nout // d,),
        in_specs=[pl.BlockSpec((n, d), lambda j: (0, 0)),
                  pl.BlockSpec((d, d), lambda j: (0, j)),
                  pl.BlockSpec((1, d), lambda j: (0, j))],
        out_specs=pl.BlockSpec((n, d), lambda j: (0, j)),
        out_shape=jax.ShapeDtypeStruct((n, nout), F32),
        compiler_params=_cparams(("arbitrary",), VMEM_LIMIT),
        name="ada_mod",
    )(c, w_ada, b_ada.reshape(1, nout))


def _inproj_kernel(x_ref, mod_ref, g_ref, w_ref, q_ref, k_ref, v_ref, u_ref, kf_ref, vf_ref, *, first_keep, pad):
    j = pl.program_id(1)

    def compute():
        x = x_ref[0]
        h = _rms(x, g_ref[...]) * (1.0 + mod_ref[0, 1:2, :]) + mod_ref[0, 0:1, :]
        proj = jnp.dot(h.astype(BF16), w_ref[...], preferred_element_type=F32)
        q_ref[0] = (proj[:, :D_ATTN] * (HEAD_DIM ** -0.5)).astype(BF16)
        k_ref[0] = proj[:, D_ATTN:2 * D_ATTN].astype(BF16)
        v_ref[0] = proj[:, 2 * D_ATTN:3 * D_ATTN].astype(BF16)
        u_ref[...] = proj[:, 3 * D_ATTN:].astype(BF16)

        @pl.when(j >= first_keep + pad)
        def _():
            kf_ref[0] = proj[:, D_ATTN:2 * D_ATTN]
            vf_ref[0] = proj[:, 2 * D_ATTN:3 * D_ATTN]

    if pad:
        @pl.when(j == 0)
        def _():
            k_ref[...] = jnp.zeros_like(k_ref)
            v_ref[...] = jnp.zeros_like(v_ref)

        pl.when(j > 0)(compute)
    else:
        compute()


def _inproj(x, mod, g, w_in_bf, *, pad_front):
    b, l, d = x.shape
    tm = min(ROW_TILE, l)
    nt = l // tm
    keep = min(WIN_ROWS, l)
    first_keep = (l - keep) // tm
    pad = 1 if pad_front else 0
    d_ssm = w_in_bf.shape[1] - 3 * D_ATTN
    assert l % tm == 0 and keep % tm == 0 and (not pad_front or tm == WIN_ROWS)

    def xi(bi, j):
        return (bi, jnp.maximum(j - pad, 0), 0)

    def kfi(bi, j):
        return (bi, jnp.maximum(j - pad - first_keep, 0), 0)

    kern = functools.partial(_inproj_kernel, first_keep=first_keep, pad=pad)
    return pl.pallas_call(
        kern,
        grid=(b, nt + pad),
        in_specs=[pl.BlockSpec((1, tm, d), xi),
                  pl.BlockSpec((1, 6, d), lambda bi, j: (bi, 0, 0)),
                  pl.BlockSpec((1, d), lambda bi, j: (0, 0)),
                  pl.BlockSpec(w_in_bf.shape, lambda bi, j: (0, 0))],
        out_specs=[pl.BlockSpec((1, tm, D_ATTN), xi),
                   pl.BlockSpec((1, tm, D_ATTN), lambda bi, j: (bi, j, 0)),
                   pl.BlockSpec((1, tm, D_ATTN), lambda bi, j: (bi, j, 0)),
                   pl.BlockSpec((tm, d_ssm), lambda bi, j: (jnp.maximum(j - pad, 0), bi)),
                   pl.BlockSpec((1, tm, D_ATTN), kfi),
                   pl.BlockSpec((1, tm, D_ATTN), kfi)],
        out_shape=[jax.ShapeDtypeStruct((b, l, D_ATTN), BF16),
                   jax.ShapeDtypeStruct((b, l + pad * tm, D_ATTN), BF16),
                   jax.ShapeDtypeStruct((b, l + pad * tm, D_ATTN), BF16),
                   jax.ShapeDtypeStruct((l, b * d_ssm), BF16),
                   jax.ShapeDtypeStruct((b, keep, D_ATTN), F32),
                   jax.ShapeDtypeStruct((b, keep, D_ATTN), F32)],
        compiler_params=_cparams(("arbitrary", "arbitrary"), VMEM_LIMIT),
        name="prenorm_inproj",
    )(x, mod, g.reshape(1, d), w_in_bf)


def _bias_kernel(rb_ref, o_ref, *, q0, k0, nq, nk, strips):
    h = pl.program_id(0)
    qpos = q0 + lax.broadcasted_iota(I32, (nq, nk), 0)
    kpos = k0 + lax.broadcasted_iota(I32, (nq, nk), 1)
    rel = jnp.clip(kpos - qpos, -MAX_REL, MAX_REL) + MAX_REL
    qc = qpos >> CHUNK_LOG2
    kc = kpos >> CHUNK_LOG2
    allowed = (kc <= qc) & (kc >= qc - N_PAST_CHUNKS)

    def body(r, acc):
        return jnp.where(rel == r, rb_ref[r, h], acc)

    val = lax.fori_loop(0, N_REL, body, jnp.zeros((nq, nk), F32))
    strip = jnp.where(allowed, val, NEG_INF)
    o_ref[0, 0:nq, :] = strip
    for c in range(1, strips):
        o_ref[0, c * nq:(c + 1) * nq, :] = pltpu.roll(strip, c * CHUNK, axis=1)


def _bias_table(rel_bias, *, q0, k0, nq, nk, strips=1):
    kern = functools.partial(_bias_kernel, q0=q0, k0=k0, nq=nq, nk=nk, strips=strips)
    return pl.pallas_call(
        kern,
        grid=(N_HEADS,),
        in_specs=[pl.BlockSpec(memory_space=pltpu.SMEM)],
        out_specs=pl.BlockSpec((1, strips * nq, nk), lambda h: (h, 0, 0)),
        out_shape=jax.ShapeDtypeStruct((N_HEADS, strips * nq, nk), F32),
        compiler_params=_cparams(("arbitrary",)),
        name="bias_table",
    )(rel_bias)


def _softmax_pv(scores, values):
    m = scores[0].max(axis=-1, keepdims=True)
    for s in scores[1:]:
        m = jnp.maximum(m, s.max(axis=-1, keepdims=True))
    acc = None
    den = None
    for s, v in zip(scores, values):
        p = jnp.exp(s - m)
        d = p.sum(axis=-1, keepdims=True)
        o = jnp.dot(p.astype(BF16), v, preferred_element_type=F32)
        acc = o if acc is None else acc + o
        den = d if den is None else den + d
    return acc / den


def _attn_prompt_kernel(q_ref, k_ref, v_ref, tab_ref, o_ref, *, tq, band):
    qi = pl.program_id(1)
    t0 = pl.multiple_of(qi * tq, tq)
    lane = lax.broadcasted_iota(I32, (1, LANES), 1)
    col = lax.broadcasted_iota(I32, (1, band), 1)
    colmask = jnp.where(col + t0 >= WIN_ROWS, 0.0, NEG_INF).astype(F32)
    for hp in range(N_HEADS // 2):
        ls = slice(hp * LANES, (hp + 1) * LANES)
        q2 = q_ref[0, :, ls]
        k2 = k_ref[0, pl.ds(t0, band), ls]
        v2 = v_ref[0, pl.ds(t0, band), ls]
        outs = []
        for a in range(2):
            qa = jnp.where((lane < HEAD_DIM) == (a == 0), q2, jnp.zeros_like(q2))
            s = lax.dot_general(qa, k2, (((1,), (1,)), ((), ())), preferred_element_type=F32)
            s = s + tab_ref[2 * hp + a] + colmask
            outs.append(_softmax_pv([s], [v2]))
        o_ref[0, :, ls] = jnp.where(lane < HEAD_DIM, outs[0], outs[1]).astype(BF16)


def _attn_prompt(q, k_pad, v_pad, table):
    b, l, _ = q.shape
    tq = Q_TILE
    band = tq + WIN_ROWS
    kern = functools.partial(_attn_prompt_kernel, tq=tq, band=band)
    return pl.pallas_call(
        kern,
        grid=(b, l // tq),
        in_specs=[pl.BlockSpec((1, tq, D_ATTN), lambda bi, i: (bi, i, 0)),
                  pl.BlockSpec((1, l + WIN_ROWS, D_ATTN), lambda bi, i: (bi, 0, 0)),
                  pl.BlockSpec((1, l + WIN_ROWS, D_ATTN), lambda bi, i: (bi, 0, 0)),
                  pl.BlockSpec(table.shape, lambda bi, i: (0, 0, 0))],
        out_specs=pl.BlockSpec((1, tq, D_ATTN), lambda bi, i: (bi, i, 0)),
        out_shape=jax.ShapeDtypeStruct((b, l, D_ATTN), BF16),
        compiler_params=_cparams(("arbitrary", "arbitrary"), VMEM_LIMIT),
        name="attn_prompt",
    )(q, k_pad, v_pad, table)


def _attn_sample_kernel(q_ref, k_ref, v_ref, ck_ref, cv_ref, tabc_ref, tabn_ref, o_ref):
    lane = lax.broadcasted_iota(I32, (1, LANES), 1)
    for hp in range(N_HEADS // 2):
        ls = slice(hp * LANES, (hp + 1) * LANES)
        q2 = q_ref[0, :, ls]
        kn = k_ref[0, :, ls]
        vn = v_ref[0, :, ls]
        kc = ck_ref[0, :, ls].astype(BF16)
        vc = cv_ref[0, :, ls].astype(BF16)
        outs = []
        for a in range(2):
            h = 2 * hp + a
            qa = jnp.where((lane < HEAD_DIM) == (a == 0), q2, jnp.zeros_like(q2))
            sc = lax.dot_general(qa, kc, (((1,), (1,)), ((), ())), preferred_element_type=F32) + tabc_ref[h]
            sn = lax.dot_general(qa, kn, (((1,), (1,)), ((), ())), preferred_element_type=F32) + tabn_ref[h]
            outs.append(_softmax_pv([sc, sn], [vc, vn]))
        o_ref[0, :, ls] = jnp.where(lane < HEAD_DIM, outs[0], outs[1]).astype(BF16)


def _attn_sample(q, k, v, cache_k, cache_v, tabc, tabn):
    b, s, _ = q.shape
    nc = cache_k.shape[1]
    return pl.pallas_call(
        _attn_sample_kernel,
        grid=(b,),
        in_specs=[pl.BlockSpec((1, s, D_ATTN), lambda bi: (bi, 0, 0)),
                  pl.BlockSpec((1, s, D_ATTN), lambda bi: (bi, 0, 0)),
                  pl.BlockSpec((1, s, D_ATTN), lambda bi: (bi, 0, 0)),
                  pl.BlockSpec((1, nc, D_ATTN), lambda bi: (bi, 0, 0)),
                  pl.BlockSpec((1, nc, D_ATTN), lambda bi: (bi, 0, 0)),
                  pl.BlockSpec(tabc.shape, lambda bi: (0, 0, 0)),
                  pl.BlockSpec(tabn.shape, lambda bi: (0, 0, 0))],
        out_specs=pl.BlockSpec((1, s, D_ATTN), lambda bi: (bi, 0, 0)),
        out_shape=jax.ShapeDtypeStruct((b, s, D_ATTN), BF16),
        compiler_params=_cparams(("arbitrary",), VMEM_LIMIT),
        name="attn_sample",
    )(q, k, v, cache_k, cache_v, tabc, tabn)


def _s5_prep_kernel(are_ref, aim_ref, ldt_ref, bre_ref, bim_ref, abre_ref, abim_ref, bbre_ref, bbim_ref):
    a_re = are_ref[...]
    a_im = aim_ref[...]
    dt = jnp.exp(ldt_ref[...])
    mag = jnp.exp(dt * a_re)
    abar_re = mag * jnp.cos(dt * a_im)
    abar_im = mag * jnp.sin(dt * a_im)
    den = a_re * a_re + a_im * a_im
    num_re = abar_re - 1.0
    coef_re = (num_re * a_re + abar_im * a_im) / den
    coef_im = (abar_im * a_re - num_re * a_im) / den
    abre_ref[...] = abar_re
    abim_ref[...] = abar_im
    b_re = bre_ref[...]
    b_im = bim_ref[...]
    bbre_ref[...] = coef_re * b_re - coef_im * b_im
    bbim_ref[...] = coef_re * b_im + coef_im * b_re


def _s5_prep(a_re, a_im, log_dt, b_re, b_im):
    g, p = a_re.shape
    bt_re = jnp.transpose(b_re, (0, 2, 1))
    bt_im = jnp.transpose(b_im, (0, 2, 1))
    i = bt_re.shape[1]
    sds = jax.ShapeDtypeStruct
    return pl.pallas_call(
        _s5_prep_kernel,
        out_shape=[sds((g, 1, p), F32), sds((g, 1, p), F32), sds((g, i, p), F32), sds((g, i, p), F32)],
        name="s5_discretise",
    )(a_re.reshape(g, 1, p), a_im.reshape(g, 1, p), log_dt.reshape(g, 1, 1), bt_re, bt_im)


def _s5_scan_kernel(u_ref, bb_ref, cc_ref, d_ref, are_ref, aim_ref, s0_ref, y_ref, sf_ref, bu_ref, st_ref,
                    *, nb, tc):
    t_step = pl.program_id(1)
    half = bu_ref.shape[1] // 2

    @pl.when(t_step == 0)
    def _():
        st_ref[...] = s0_ref[0]

    u = u_ref[...]
    bu_ref[...] = jnp.dot(u, bb_ref[0], preferred_element_type=F32)
    a_re = jnp.broadcast_to(are_ref[0], (nb, half))
    a_im = jnp.broadcast_to(aim_ref[0], (nb, half))

    def step(t, carry):
        x_re, x_im = carry
        rows = pl.ds(pl.multiple_of(t * nb, nb), nb)
        n_re = a_re * x_re - a_im * x_im + bu_ref[rows, 0:half]
        n_im = a_re * x_im + a_im * x_re + bu_ref[rows, half:2 * half]
        bu_ref[rows, 0:half] = n_re
        bu_ref[rows, half:2 * half] = n_im
        return n_re, n_im

    x_re, x_im = lax.fori_loop(0, tc, step, (st_ref[:, 0:half], st_ref[:, half:2 * half]), unroll=2)
    st_ref[:, 0:half] = x_re
    st_ref[:, half:2 * half] = x_im
    sf_ref[0] = st_ref[...]
    y = jnp.dot(bu_ref[...].astype(BF16), cc_ref[0], preferred_element_type=F32)
    y = y + d_ref[0] * u.astype(F32)
    y_ref[...] = jax.nn.gelu(y).astype(BF16)


def _s5_scan(u_tm, bb, cc, dvec, abar_re, abar_im, s0, *, nb, l):
    d_ssm = u_tm.shape[1]
    npk = bb.shape[0]
    cin = bb.shape[1]
    two_half = bb.shape[2]
    tc = min(SSM_TIME, l)
    rows = tc * nb
    kern = functools.partial(_s5_scan_kernel, nb=nb, tc=tc)
    return pl.pallas_call(
        kern,
        grid=(npk, l // tc),
        in_specs=[pl.BlockSpec((rows, cin), lambda p, t: (t, p)),
                  pl.BlockSpec((1, cin, two_half), lambda p, t: (p, 0, 0)),
                  pl.BlockSpec((1, two_half, cin), lambda p, t: (p, 0, 0)),
                  pl.BlockSpec((1, 1, cin), lambda p, t: (p, 0, 0)),
                  pl.BlockSpec((1, 1, two_half // 2), lambda p, t: (p, 0, 0)),
                  pl.BlockSpec((1, 1, two_half // 2), lambda p, t: (p, 0, 0)),
                  pl.BlockSpec((1, nb, two_half), lambda p, t: (p, 0, 0))],
        out_specs=[pl.BlockSpec((rows, cin), lambda p, t: (t, p)),
                   pl.BlockSpec((1, nb, two_half), lambda p, t: (p, 0, 0))],
        out_shape=[jax.ShapeDtypeStruct((l * nb, d_ssm), BF16),
                   jax.ShapeDtypeStruct((npk, nb, two_half), F32)],
        scratch_shapes=[pltpu.VMEM((rows, two_half), F32), pltpu.VMEM((nb, two_half), F32)],
        compiler_params=_cparams(("arbitrary", "arbitrary"), VMEM_LIMIT),
        name="s5_scan",
    )(u_tm, bb, cc, dvec, abar_re, abar_im, s0)


def _s5_pack_weights(bb_re, bb_im, c_re, c_im, d, abar_re, abar_im):
    g, i, p = bb_re.shape
    npk = g // SSM_PACK
    eye = jnp.eye(SSM_PACK, dtype=F32)

    def blockdiag_in(m):
        m = m.reshape(npk, SSM_PACK, i, p)
        return jnp.einsum("ngip,gh->ngihp", m, eye).reshape(npk, SSM_PACK * i, SSM_PACK * p)

    def blockdiag_out(m):
        m = m.reshape(npk, SSM_PACK, i, p)
        return jnp.einsum("ngip,gh->ngphi", m, eye).reshape(npk, SSM_PACK * p, SSM_PACK * i)

    bb = jnp.concatenate([blockdiag_in(bb_re), blockdiag_in(bb_im)], axis=2).astype(BF16)
    cc = jnp.concatenate([blockdiag_out(c_re), -blockdiag_out(c_im)], axis=1).astype(BF16)
    dvec = d.reshape(npk, 1, SSM_PACK * i)
    are = abar_re.reshape(npk, 1, SSM_PACK * p)
    aim = abar_im.reshape(npk, 1, SSM_PACK * p)
    return bb, cc, dvec, are, aim


def _state_to_packs(s_re, s_im):
    b, g, p = s_re.shape
    npk = g // SSM_PACK
    re = jnp.transpose(s_re.reshape(b, npk, SSM_PACK * p), (1, 0, 2))
    im = jnp.transpose(s_im.reshape(b, npk, SSM_PACK * p), (1, 0, 2))
    return jnp.concatenate([re, im], axis=2)


def _packs_to_state(sf, g, p):
    npk, b, two_half = sf.shape
    half = two_half // 2
    re = jnp.transpose(sf[:, :, :half], (1, 0, 2)).reshape(b, g, p)
    im = jnp.transpose(sf[:, :, half:], (1, 0, 2)).reshape(b, g, p)
    return re, im


def _outproj_kernel(attn_ref, ssm_ref, x_ref, mod_ref, wglu_ref, bglu_ref, wout_ref, gpm_ref, gpf_ref,
                    wr_ref, br_ref, x1_ref, h2_ref, lg_ref):
    y = ssm_ref[...]
    z = jnp.dot(y, wglu_ref[...], preferred_element_type=F32) + bglu_ref[...]
    s = (y.astype(F32) * jax.nn.sigmoid(z)).astype(BF16)
    m = jnp.dot(attn_ref[0], wout_ref[0:D_ATTN, :], preferred_element_type=F32)
    m = m + jnp.dot(s, wout_ref[D_ATTN:, :], preferred_element_type=F32)
    x1 = x_ref[0] + mod_ref[0, 2:3, :] * _rms(m, gpm_ref[...])
    x1_ref[0] = x1
    h2 = (_rms(x1, gpf_ref[...]) * (1.0 + mod_ref[0, 4:5, :]) + mod_ref[0, 3:4, :]).astype(BF16)
    h2_ref[...] = h2
    lg_ref[...] = jnp.dot(h2, wr_ref[...], preferred_element_type=F32) + br_ref[...]


def _outproj(attn, ssm2d, x, mod, w_glu_bf, b_glu, w_out_bf, g_post_mix, g_pre_ffn, w_router_bf, b_router):
    b, l, d = x.shape
    tm = min(ROW_TILE, l)
    nt = l // tm
    d_ssm = w_glu_bf.shape[0]
    ne = w_router_bf.shape[1]
    c2 = lambda bi, j: (0, 0)
    return pl.pallas_call(
        _outproj_kernel,
        grid=(b, nt),
        in_specs=[pl.BlockSpec((1, tm, D_ATTN), lambda bi, j: (bi, j, 0)),
                  pl.BlockSpec((tm, d_ssm), lambda bi, j: (j, bi)),
                  pl.BlockSpec((1, tm, d), lambda bi, j: (bi, j, 0)),
                  pl.BlockSpec((1, 6, d), lambda bi, j: (bi, 0, 0)),
                  pl.BlockSpec(w_glu_bf.shape, c2),
                  pl.BlockSpec((1, d_ssm), c2),
                  pl.BlockSpec(w_out_bf.shape, c2),
                  pl.BlockSpec((1, d), c2),
                  pl.BlockSpec((1, d), c2),
                  pl.BlockSpec(w_router_bf.shape, c2),
                  pl.BlockSpec((1, ne), c2)],
        out_specs=[pl.BlockSpec((1, tm, d), lambda bi, j: (bi, j, 0)),
                   pl.BlockSpec((tm, d), lambda bi, j: (bi * nt + j, 0)),
                   pl.BlockSpec((tm, ne), lambda bi, j: (bi * nt + j, 0))],
        out_shape=[jax.ShapeDtypeStruct((b, l, d), F32),
                   jax.ShapeDtypeStruct((b * l, d), BF16),
                   jax.ShapeDtypeStruct((b * l, ne), F32)],
        compiler_params=_cparams(("arbitrary", "arbitrary"), VMEM_LIMIT),
        name="glu_outproj_norms_router",
    )(attn, ssm2d, x, mod, w_glu_bf, b_glu.reshape(1, d_ssm), w_out_bf, g_post_mix.reshape(1, d),
      g_pre_ffn.reshape(1, d), w_router_bf, b_router.reshape(1, ne))


def _router_kernel(lg_ref, tri_ref, info_ref, pc_ref):
    lg = lg_ref[...]
    ne, tb = lg.shape
    eidx = lax.broadcasted_iota(I32, (ne, tb), 0).astype(F32)
    work = lg
    sels, vals = [], []
    for _ in range(TOP_K):
        m = work.max(axis=0, keepdims=True)
        idx = jnp.where(work == m, eidx, float(ne)).min(axis=0, keepdims=True)
        sel = eidx == idx
        sels.append(sel)
        vals.append(m)
        work = jnp.where(sel, -jnp.inf, work)
    ex = [jnp.exp(v - vals[0]) for v in vals]
    den = ex[0] + ex[1] + ex[2] + ex[3]
    onehot = (sels[0] | sels[1] | sels[2] | sels[3]).astype(F32)
    cnt = onehot.sum(axis=1, keepdims=True).astype(I32)
    pc = ((cnt + (BF16_ROWS - 1)) >> 4) << 4
    pc_ref[0] = pc
    rank = jnp.dot(onehot.astype(BF16), tri_ref[...], preferred_element_type=F32)
    low = jnp.where(lax.broadcasted_iota(I32, (ne, LANES), 1) < lax.broadcasted_iota(I32, (ne, LANES), 0), 1.0, 0.0)
    pcb = jnp.concatenate([jnp.broadcast_to(pc.astype(F32), (ne, LANES)), jnp.zeros((LANES - ne, LANES), F32)], axis=0)
    off = jnp.dot(low.astype(BF16), pcb.astype(BF16), preferred_element_type=F32)[:, 0:1]
    posmat = rank + off
    for k in range(TOP_K):
        info_ref[k:k + 1, :] = jnp.where(sels[k], posmat, 0.0).sum(axis=0, keepdims=True)
        info_ref[TOP_K + k:TOP_K + k + 1, :] = ex[k] / den


def _router(logits_t, tb):
    ne, t = logits_t.shape
    nb = t // tb
    tri = jnp.triu(jnp.ones((tb, tb), BF16), k=1)
    return pl.pallas_call(
        _router_kernel,
        grid=(nb,),
        in_specs=[pl.BlockSpec((ne, tb), lambda i: (0, i)),
                  pl.BlockSpec((tb, tb), lambda i: (0, 0))],
        out_specs=[pl.BlockSpec((2 * TOP_K, tb), lambda i: (0, i)),
                   pl.BlockSpec((1, ne, 1), lambda i: (i, 0, 0))],
        out_shape=[jax.ShapeDtypeStruct((2 * TOP_K, t), F32),
                   jax.ShapeDtypeStruct((nb, ne, 1), I32)],
        compiler_params=_cparams(("arbitrary",)),
        name="router_top4",
    )(logits_t, tri)


def _local_chunks(tb, ne):
    rows = TOP_K * tb + ne * (BF16_ROWS - 1)
    return -(-rows // LANES) * (LANES // BF16_ROWS)


def _sort_plan(pc, tb):
    nb, ne = pc.shape
    nch = _local_chunks(tb, ne)
    n_tiles = (TOP_K * tb * nb + nb * ne * (BF16_ROWS - 1)) // FFN_TILE + ne
    loc_end = jnp.cumsum(pc, axis=1)
    loc_off = loc_end - pc
    tot = pc.sum(axis=0)
    tot_pad = ((tot + FFN_TILE - 1) // FFN_TILE) * FFN_TILE
    base_end = jnp.cumsum(tot_pad)
    base = base_end - tot_pad
    glob = base[None, :] + jnp.cumsum(pc, axis=0) - pc
    n_used = (base_end[-1] // FFN_TILE).astype(I32)
    tile_expert = jnp.minimum(
        jnp.searchsorted(base_end, jnp.arange(n_tiles, dtype=I32) * FFN_TILE, side="right"), ne - 1).astype(I32)
    r = jnp.arange(nch, dtype=I32) * BF16_ROWS
    e_of = jnp.minimum(jax.vmap(lambda ends: jnp.searchsorted(ends, r, side="right"))(loc_end), ne - 1)
    dst = jnp.take_along_axis(glob - loc_off, e_of, axis=1) + r[None, :]
    chunk_dst = (dst // BF16_ROWS).astype(I32)
    n_valid = (loc_end[:, -1] // BF16_ROWS).astype(I32)
    tail_n = ((tot_pad - tot) // BF16_ROWS).astype(I32)
    tail_dst = ((base + tot) // BF16_ROWS).astype(I32)
    return dict(nch=nch, n_tiles=n_tiles, chunk_dst=chunk_dst.reshape(-1), n_valid=n_valid, n_used=n_used,
                tile_expert=tile_expert, tail_n=tail_n, tail_dst=tail_dst)


def _chunk_copy(buf, hbm, slot, c, dst_chunk, sem, to_hbm):
    v = buf.at[slot, pl.ds(pl.multiple_of(c * BF16_ROWS, BF16_ROWS), BF16_ROWS)]
    h = hbm.at[pl.ds(pl.multiple_of(dst_chunk * BF16_ROWS, BF16_ROWS), BF16_ROWS)]
    return pltpu.make_async_copy(v, h, sem) if to_hbm else pltpu.make_async_copy(h, v, sem)


def _dispatch_kernel(cd_ref, nv_ref, tn_ref, td_ref, nu_ref, h_ref, info_ref, xs_ref, buf_ref, zero_ref, ztile_ref,
                     sem, zsem, *, nch, rt, n_tiles):
    b = pl.program_id(0)
    nb = pl.num_programs(0)
    slot = b % 2
    tb = h_ref.shape[0]
    h = h_ref[...]
    for r0 in range(0, nch * BF16_ROWS, rt):
        rn = min(rt, nch * BF16_ROWS - r0)
        rid = (r0 + lax.broadcasted_iota(I32, (rn, tb), 0)).astype(F32)
        hit = rid == info_ref[0:1, :]
        for k in range(1, TOP_K):
            hit = hit | (rid == info_ref[k:k + 1, :])
        p = jnp.where(hit, 1.0, 0.0).astype(BF16)
        buf_ref[slot, r0:r0 + rn, :] = jnp.dot(p, h, preferred_element_type=F32).astype(BF16)

    def issue(c, _):
        _chunk_copy(buf_ref, xs_ref, slot, c, cd_ref[b * nch + c], sem.at[slot], True).start()
        return 0

    lax.fori_loop(0, nv_ref[b], issue, 0)

    def drain(step, sl):
        def wait(c, _):
            _chunk_copy(buf_ref, xs_ref, sl, c, cd_ref[step * nch + c], sem.at[sl], True).wait()
            return 0

        lax.fori_loop(0, nv_ref[step], wait, 0)

    @pl.when(b > 0)
    def _():
        drain(b - 1, 1 - slot)

    @pl.when(b == nb - 1)
    def _():
        drain(b, slot)
        zero_ref[...] = jnp.zeros_like(zero_ref)
        for e in range(N_EXPERTS):
            def zcopy(c, e=e):
                dst = pl.multiple_of((td_ref[e] + c) * BF16_ROWS, BF16_ROWS)
                return pltpu.make_async_copy(zero_ref, xs_ref.at[pl.ds(dst, BF16_ROWS)], zsem)

            lax.fori_loop(0, tn_ref[e], lambda c, _: (zcopy(c).start(), 0)[1], 0)
            lax.fori_loop(0, tn_ref[e], lambda c, _: (zcopy(c).wait(), 0)[1], 0)

        ztile_ref[...] = jnp.zeros_like(ztile_ref)

        def ztile(i):
            dst = pl.multiple_of(i * FFN_TILE, FFN_TILE)
            return pltpu.make_async_copy(ztile_ref, xs_ref.at[pl.ds(dst, FFN_TILE)], zsem)

        lax.fori_loop(nu_ref[0], n_tiles, lambda i, _: (ztile(i).start(), 0)[1], 0)
        lax.fori_loop(nu_ref[0], n_tiles, lambda i, _: (ztile(i).wait(), 0)[1], 0)


def _dispatch(h2, info, plan, tb):
    t, d = h2.shape
    nb = t // tb
    nch = plan["nch"]
    rows = plan["n_tiles"] * FFN_TILE
    rt = 512
    kern = functools.partial(_dispatch_kernel, nch=nch, rt=rt, n_tiles=plan["n_tiles"])
    grid_spec = pltpu.PrefetchScalarGridSpec(
        num_scalar_prefetch=5,
        grid=(nb,),
        in_specs=[pl.BlockSpec((tb, d), lambda i, *_: (i, 0)),
                  pl.BlockSpec((2 * TOP_K, tb), lambda i, *_: (0, i))],
        out_specs=pl.BlockSpec(memory_space=pl.ANY),
        scratch_shapes=[pltpu.VMEM((2, nch * BF16_ROWS, d), BF16),
                        pltpu.VMEM((BF16_ROWS, d), BF16),
                        pltpu.VMEM((FFN_TILE, d), BF16),
                        pltpu.SemaphoreType.DMA((2,)),
                        pltpu.SemaphoreType.DMA],
    )
    return pl.pallas_call(
        kern,
        grid_spec=grid_spec,
        out_shape=jax.ShapeDtypeStruct((rows, d), BF16),
        compiler_params=_cparams(("arbitrary",), VMEM_LIMIT),
        name="moe_dispatch",
    )(plan["chunk_dst"], plan["n_valid"], plan["tail_n"], plan["tail_dst"], plan["n_used"].reshape(1), h2, info)


def _ffn_kernel(te_ref, nu_ref, x_ref, wgu_ref, bgu_ref, wd_ref, bd_ref, perm_ref, y_ref, wgu_s, wd_s):
    i = pl.program_id(0)
    f = wd_s.shape[0]
    prev = te_ref[jnp.maximum(i - 1, 0)]
    new_expert = (i == 0) | (te_ref[i] != prev)

    @pl.when((i < nu_ref[0]) & new_expert)
    def _():
        for c in range(0, 2 * f, 2 * LANES):
            blk = wgu_ref[0, :, c:c + 2 * LANES].astype(BF16)
            wgu_s[:, c:c + 2 * LANES] = jnp.dot(blk, perm_ref[...], preferred_element_type=F32).astype(BF16)
        wd_s[...] = wd_ref[0].astype(BF16)

    @pl.when(i < nu_ref[0])
    def _():
        x = x_ref[...]
        acts = []
        for c in range(0, 2 * f, 2 * LANES):
            gu = jnp.dot(x, wgu_s[:, c:c + 2 * LANES], preferred_element_type=F32) + bgu_ref[0, :, c:c + 2 * LANES]
            glu = jnp.minimum(gu[:, :LANES], SWIGLU_LIMIT)
            lin = jnp.clip(gu[:, LANES:], -SWIGLU_LIMIT, SWIGLU_LIMIT)
            acts.append((glu * jax.nn.sigmoid(SWIGLU_ALPHA * glu) * (lin + 1.0)).astype(BF16))
        act = jnp.concatenate(acts, axis=1)
        y_ref[...] = (jnp.dot(act, wd_s[...], preferred_element_type=F32) + bd_ref[0]).astype(BF16)

    @pl.when(i >= nu_ref[0])
    def _():
        y_ref[...] = jnp.zeros_like(y_ref)


def _ffn(xs, plan, w_gate_up, b_gu_perm, w_down, b_down):
    rows, d = xs.shape
    ne, _, f2 = w_gate_up.shape
    f = f2 // 2
    n_tiles = plan["n_tiles"]
    src = jnp.concatenate([jnp.arange(LANES) * 2, jnp.arange(LANES) * 2 + 1])
    perm = (jnp.arange(2 * LANES)[:, None] == src[None, :]).astype(BF16)

    def xi(i, te, nu):
        return (jnp.minimum(i, nu[0] - 1), 0)

    def wi(i, te, nu):
        return (te[jnp.minimum(i, nu[0] - 1)], 0, 0)

    grid_spec = pltpu.PrefetchScalarGridSpec(
        num_scalar_prefetch=2,
        grid=(n_tiles,),
        in_specs=[pl.BlockSpec((FFN_TILE, d), xi),
                  pl.BlockSpec((1, d, f2), wi),
                  pl.BlockSpec((1, 1, f2), wi),
                  pl.BlockSpec((1, f, d), wi),
                  pl.BlockSpec((1, 1, d), wi),
                  pl.BlockSpec((2 * LANES, 2 * LANES), lambda i, te, nu: (0, 0))],
        out_specs=pl.BlockSpec((FFN_TILE, d), lambda i, te, nu: (i, 0)),
        scratch_shapes=[pltpu.VMEM((d, f2), BF16), pltpu.VMEM((f, d), BF16)],
    )
    return pl.pallas_call(
        _ffn_kernel,
        grid_spec=grid_spec,
        out_shape=jax.ShapeDtypeStruct((rows, d), BF16),
        compiler_params=_cparams(("arbitrary",), VMEM_LIMIT),
        name="moe_expert_ffn",
    )(plan["tile_expert"], plan["n_used"].reshape(1), xs, w_gate_up, b_gu_perm.reshape(ne, 1, f2), w_down,
      b_down.reshape(ne, 1, d), perm)


def _combine_kernel(cd_ref, nv_ref, ys_ref, info_ref, x1_ref, mod_ref, g_ref, o_ref, buf_ref, w_ref, sem,
                    *, nch, ct):
    b = pl.program_id(0)
    tb = x1_ref.shape[0]
    nbat = mod_ref.shape[0]
    rb = tb // nbat

    @pl.when(b == 0)
    def _():
        buf_ref[...] = jnp.zeros_like(buf_ref)

    def fetch(c, _):
        _chunk_copy(buf_ref, ys_ref, 0, c, cd_ref[b * nch + c], sem, False).start()
        return 0

    lax.fori_loop(0, nv_ref[b], fetch, 0)

    for c0 in range(0, nch * BF16_ROWS, ct):
        cn = min(ct, nch * BF16_ROWS - c0)
        cid = (c0 + lax.broadcasted_iota(I32, (tb, cn), 1)).astype(F32)
        w = jnp.where(cid == info_ref[:, 0:1], info_ref[:, TOP_K:TOP_K + 1], 0.0)
        for k in range(1, TOP_K):
            w = w + jnp.where(cid == info_ref[:, k:k + 1], info_ref[:, TOP_K + k:TOP_K + k + 1], 0.0)
        w_ref[:, c0:c0 + cn] = w.astype(BF16)

    def wait(c, _):
        _chunk_copy(buf_ref, ys_ref, 0, c, cd_ref[b * nch + c], sem, False).wait()
        return 0

    lax.fori_loop(0, nv_ref[b], wait, 0)

    moe = jnp.dot(w_ref[...], buf_ref[0], preferred_element_type=F32)
    out = _rms(moe, g_ref[...])
    for s in range(nbat):
        rs = slice(s * rb, (s + 1) * rb)
        o_ref[rs, :] = x1_ref[rs, :] + mod_ref[s, 5:6, :] * out[rs, :]


def _combine(ys, info_t, x1, mod, g_post_ffn, plan, tb, l):
    t, d = x1.shape
    nb = t // tb
    nch = plan["nch"]
    ct = 512
    if l >= tb:
        nbat, mi = 1, (lambda i, *_: (i // (l // tb), 0, 0))
    else:
        nbat, mi = tb // l, (lambda i, *_: (i, 0, 0))
    kern = functools.partial(_combine_kernel, nch=nch, ct=ct)
    grid_spec = pltpu.PrefetchScalarGridSpec(
        num_scalar_prefetch=2,
        grid=(nb,),
        in_specs=[pl.BlockSpec(memory_space=pl.ANY),
                  pl.BlockSpec((tb, 2 * TOP_K), lambda i, *_: (i, 0)),
                  pl.BlockSpec((tb, d), lambda i, *_: (i, 0)),
                  pl.BlockSpec((nbat, 6, d), mi),
                  pl.BlockSpec((1, d), lambda i, *_: (0, 0))],
        out_specs=pl.BlockSpec((tb, d), lambda i, *_: (i, 0)),
        scratch_shapes=[pltpu.VMEM((1, nch * BF16_ROWS, d), BF16), pltpu.VMEM((tb, nch * BF16_ROWS), BF16),
                        pltpu.SemaphoreType.DMA],
    )
    return pl.pallas_call(
        kern,
        grid_spec=grid_spec,
        out_shape=jax.ShapeDtypeStruct((t, d), F32),
        compiler_params=_cparams(("arbitrary",), VMEM_LIMIT),
        name="moe_combine",
    )(plan["chunk_dst"], plan["n_valid"], ys, info_t, x1, mod, g_post_ffn.reshape(1, d))


def _moe(h2, logits, x1, mod, l, g_post_ffn, w_gate_up, b_gu_perm, w_down, b_down):
    t, d = h2.shape
    tb = min(ROW_TILE, t)
    info, pc = _router(logits[:, :N_EXPERTS].T, tb)
    plan = _sort_plan(pc.reshape(t // tb, N_EXPERTS), tb)
    xs = _dispatch(h2, info, plan, tb)
    ys = _ffn(xs, plan, w_gate_up, b_gu_perm, w_down, b_down)
    return _combine(ys, info.T, x1, mod, g_post_ffn, plan, tb, l)


def _stream(x, mod, wts, attn_fn, s0_packs):
    b, l, d = x.shape
    q, k_att, v_att, u2d, kf, vf = _inproj(x, mod, wts["g_pre_mix"], wts["w_in"], pad_front=attn_fn is None)
    if attn_fn is None:
        attn = _attn_prompt(q, k_att, v_att, wts["table_prompt"])
    else:
        attn = attn_fn(q, k_att, v_att)
    d_ssm = u2d.shape[1] // b
    u_tm = u2d.reshape(l * b, d_ssm)
    y_tm, sf = _s5_scan(u_tm, wts["bb"], wts["cc"], wts["dvec"], wts["are"], wts["aim"], s0_packs, nb=b, l=l)
    ssm2d = y_tm.reshape(l, b * d_ssm)
    x1, h2, logits = _outproj(attn, ssm2d, x, mod, wts["w_glu"], wts["b_glu"], wts["w_out"], wts["g_post_mix"],
                              wts["g_pre_ffn"], wts["w_router"], wts["b_router"])
    y = _moe(h2, logits, x1.reshape(b * l, d), mod, l, wts["g_post_ffn"], wts["w_gate_up"], wts["b_gu_perm"],
             wts["w_down"], wts["b_down"])
    return y.reshape(b, l, d), kf, vf, sf


def kernel(x_prompt, x_sample, cache_k, cache_v, state_ssm_re, state_ssm_im, c_prompt, c_sample, w_ada, b_ada, g_pre_mix, g_post_mix, g_pre_ffn, g_post_ffn, w_in, rel_bias, ssm_a_re, ssm_a_im, ssm_log_dt, ssm_b_re, ssm_b_im, ssm_c_re, ssm_c_im, ssm_d, w_glu, b_glu, w_out, w_router, b_router, w_gate_up, b_gate_up, w_down, b_down):
    depth = w_ada.shape[0]
    assert depth == 1, "single-layer trunk"
    bp, lp, d = x_prompt.shape
    bs, ls, _ = x_sample.shape
    n_cache = cache_k.shape[2]
    past_len = PAST_LEN
    g, p = ssm_a_re.shape[1:]
    lyr = 0

    mod = _ada(jnp.concatenate([c_prompt, c_sample], axis=0), w_ada[lyr], b_ada[lyr])
    mod = mod.reshape(bp + bs, 6, d)

    abre, abim, bbre, bbim = _s5_prep(ssm_a_re[lyr], ssm_a_im[lyr], ssm_log_dt[lyr], ssm_b_re[lyr], ssm_b_im[lyr])
    bb, cc, dvec, are, aim = _s5_pack_weights(bbre, bbim, ssm_c_re[lyr], ssm_c_im[lyr], ssm_d[lyr], abre, abim)

    f2 = w_gate_up.shape[-1]
    b_gu_perm = b_gate_up[lyr].reshape(N_EXPERTS, f2 // (2 * LANES), LANES, 2)
    b_gu_perm = jnp.transpose(b_gu_perm, (0, 1, 3, 2)).reshape(N_EXPERTS, f2)

    rb = rel_bias[lyr]
    wts = dict(
        g_pre_mix=g_pre_mix[lyr], g_post_mix=g_post_mix[lyr], g_pre_ffn=g_pre_ffn[lyr], g_post_ffn=g_post_ffn[lyr],
        w_in=w_in[lyr].astype(BF16), w_glu=w_glu[lyr].astype(BF16), b_glu=b_glu[lyr], w_out=w_out[lyr].astype(BF16),
        w_router=jnp.pad(w_router[lyr], ((0, 0), (0, LANES - N_EXPERTS))).astype(BF16),
        b_router=jnp.pad(b_router[lyr], (0, LANES - N_EXPERTS)),
        w_gate_up=w_gate_up[lyr], b_gu_perm=b_gu_perm, w_down=w_down[lyr], b_down=b_down[lyr],
        bb=bb, cc=cc, dvec=dvec, are=are, aim=aim,
        table_prompt=_bias_table(rb, q0=WIN_ROWS, k0=0, nq=CHUNK, nk=Q_TILE + WIN_ROWS, strips=Q_TILE // CHUNK),
    )

    zero_state = jnp.zeros((g // SSM_PACK, bp, 2 * SSM_PACK * p), F32)
    y_p, kf_p, vf_p, sf_p = _stream(x_prompt, mod[:bp], wts, None, zero_state)

    tabc = _bias_table(rb, q0=past_len, k0=past_len - n_cache, nq=ls, nk=n_cache)
    tabn = _bias_table(rb, q0=past_len, k0=past_len, nq=ls, nk=ls)
    ck = cache_k[lyr].reshape(bs, n_cache, D_ATTN)
    cv = cache_v[lyr].reshape(bs, n_cache, D_ATTN)
    attn_s = lambda q, k, v: _attn_sample(q, k, v, ck, cv, tabc, tabn)
    s0 = _state_to_packs(state_ssm_re[lyr], state_ssm_im[lyr])
    y_s, kf_s, vf_s, sf_s = _stream(x_sample, mod[bp:], wts, attn_s, s0)

    def heads(a):
        return a.reshape(1, a.shape[0], a.shape[1], N_HEADS, HEAD_DIM)

    sre_p, sim_p = _packs_to_state(sf_p, g, p)
    sre_s, sim_s = _packs_to_state(sf_s, g, p)
    return (y_p, y_s, heads(kf_p), heads(vf_p), sre_p[None], sim_p[None],
            heads(kf_s), heads(vf_s), sre_s[None], sim_s[None])
```

```python
import functools
import math

import jax
import jax.numpy as jnp
from jax import lax
from jax.experimental import pallas as pl
from jax.experimental.pallas import tpu as pltpu

F32 = jnp.float32
BF16 = jnp.bfloat16
I32 = jnp.int32
HIGHEST = lax.Precision.HIGHEST

CHUNK = 64
CHUNK_LOG2 = 6
N_PAST_CHUNKS = 8
PAST_LEN = 2048
WIN_ROWS = N_PAST_CHUNKS * CHUNK
MAX_REL = 64
N_REL = 2 * MAX_REL + 1
N_HEADS = 8
HEAD_DIM = 64
D_ATTN = N_HEADS * HEAD_DIM
SSM_GROUP = 16
SSM_STATE = 64
N_EXPERTS = 32
TOP_K = 4
SWIGLU_LIMIT = 7.0
SWIGLU_ALPHA = 1.702
RMS_EPS = 1e-6
NEG_INF = -1e30

LANES = 128
BF16_ROWS = 16
VMEM_LIMIT = 56 * 1024 * 1024

ROW_TILE = 512
Q_TILE = 256
SSM_PACK = 8
SSM_TIME = 64
FFN_TILE = 512
SORT_ROWS = 256


def _cparams(sem, vmem=None):
    return pltpu.CompilerParams(dimension_semantics=sem, vmem_limit_bytes=vmem)


def _rms(x, g):
    return x * lax.rsqrt(jnp.mean(x * x, axis=-1, keepdims=True) + RMS_EPS) * g


def _ada_kernel(c_ref, w_ref, b_ref, o_ref):
    c = c_ref[...]
    s = c * jax.nn.sigmoid(c)
    o_ref[...] = jnp.dot(s, w_ref[...], precision=HIGHEST, preferred_element_type=F32) + b_ref[...]


def _ada(c, w_ada, b_ada):
    n, d = c.shape
    nout = w_ada.shape[1]
    return pl.pallas_call(
        _ada_kernel,
        grid=(nout // d,),
        in_specs=[pl.BlockSpec((n, d), lambda j: (0, 0)),
                  pl.BlockSpec((d, d), lambda j: (0, j)),
                  pl.BlockSpec((1, d), lambda j: (0, j))],
        out_specs=pl.BlockSpec((n, d), lambda j: (0, j)),
        out_shape=jax.ShapeDtypeStruct((n, nout), F32),
        compiler_params=_cparams(("arbitrary",), VMEM_LIMIT),
        name="ada_mod",
    )(c, w_ada, b_ada.reshape(1, nout))


def _inproj_kernel(x_ref, mod_ref, g_ref, w_ref, q_ref, k_ref, v_ref, u_ref, kf_ref, vf_ref, *, first_keep, pad):
    j = pl.program_id(1)

    def compute():
        x = x_ref[0]
        h = _rms(x, g_ref[...]) * (1.0 + mod_ref[0, 1:2, :]) + mod_ref[0, 0:1, :]
        proj = jnp.dot(h.astype(BF16), w_ref[...], preferred_element_type=F32)
        q_ref[0] = (proj[:, :D_ATTN] * (HEAD_DIM ** -0.5)).astype(BF16)
        k_ref[0] = proj[:, D_ATTN:2 * D_ATTN].astype(BF16)
        v_ref[0] = proj[:, 2 * D_ATTN:3 * D_ATTN].astype(BF16)
        u_ref[...] = proj[:, 3 * D_ATTN:].astype(BF16)

        @pl.when(j >= first_keep + pad)
        def _():
            kf_ref[0] = proj[:, D_ATTN:2 * D_ATTN]
            vf_ref[0] = proj[:, 2 * D_ATTN:3 * D_ATTN]

    if pad:
        @pl.when(j == 0)
        def _():
            k_ref[...] = jnp.zeros_like(k_ref)
            v_ref[...] = jnp.zeros_like(v_ref)

        pl.when(j > 0)(compute)
    else:
        compute()


def _inproj(x, mod, g, w_in_bf, *, pad_front):
    b, l, d = x.shape
    tm = min(ROW_TILE, l)
    nt = l // tm
    keep = min(WIN_ROWS, l)
    first_keep = (l - keep) // tm
    pad = 1 if pad_front else 0
    d_ssm = w_in_bf.shape[1] - 3 * D_ATTN
    assert l % tm == 0 and keep % tm == 0 and (not pad_front or tm == WIN_ROWS)

    def xi(bi, j):
        return (bi, jnp.maximum(j - pad, 0), 0)

    def kfi(bi, j):
        return (bi, jnp.maximum(j - pad - first_keep, 0), 0)

    kern = functools.partial(_inproj_kernel, first_keep=first_keep, pad=pad)
    return pl.pallas_call(
        kern,
        grid=(b, nt + pad),
        in_specs=[pl.BlockSpec((1, tm, d), xi),
                  pl.BlockSpec((1, 6, d), lambda bi, j: (bi, 0, 0)),
                  pl.BlockSpec((1, d), lambda bi, j: (0, 0)),
                  pl.BlockSpec(w_in_bf.shape, lambda bi, j: (0, 0))],
        out_specs=[pl.BlockSpec((1, tm, D_ATTN), xi),
                   pl.BlockSpec((1, tm, D_ATTN), lambda bi, j: (bi, j, 0)),
                   pl.BlockSpec((1, tm, D_ATTN), lambda bi, j: (bi, j, 0)),
                   pl.BlockSpec((tm, d_ssm), lambda bi, j: (jnp.maximum(j - pad, 0), bi)),
                   pl.BlockSpec((1, tm, D_ATTN), kfi),
                   pl.BlockSpec((1, tm, D_ATTN), kfi)],
        out_shape=[jax.ShapeDtypeStruct((b, l, D_ATTN), BF16),
                   jax.ShapeDtypeStruct((b, l + pad * tm, D_ATTN), BF16),
                   jax.ShapeDtypeStruct((b, l + pad * tm, D_ATTN), BF16),
                   jax.ShapeDtypeStruct((l, b * d_ssm), BF16),
                   jax.ShapeDtypeStruct((b, keep, D_ATTN), F32),
                   jax.ShapeDtypeStruct((b, keep, D_ATTN), F32)],
        compiler_params=_cparams(("arbitrary", "arbitrary"), VMEM_LIMIT),
        name="prenorm_inproj",
    )(x, mod, g.reshape(1, d), w_in_bf)


def _bias_kernel(rb_ref, o_ref, *, q0, k0, nq, nk, strips):
    h = pl.program_id(0)
    qpos = q0 + lax.broadcasted_iota(I32, (nq, nk), 0)
    kpos = k0 + lax.broadcasted_iota(I32, (nq, nk), 1)
    rel = jnp.clip(kpos - qpos, -MAX_REL, MAX_REL) + MAX_REL
    qc = qpos >> CHUNK_LOG2
    kc = kpos >> CHUNK_LOG2
    allowed = (kc <= qc) & (kc >= qc - N_PAST_CHUNKS)

    def body(r, acc):
        return jnp.where(rel == r, rb_ref[r, h], acc)

    val = lax.fori_loop(0, N_REL, body, jnp.zeros((nq, nk), F32))
    strip = jnp.where(allowed, val, NEG_INF)
    o_ref[0, 0:nq, :] = strip
    for c in range(1, strips):
        o_ref[0, c * nq:(c + 1) * nq, :] = pltpu.roll(strip, c * CHUNK, axis=1)


def _bias_table(rel_bias, *, q0, k0, nq, nk, strips=1):
    kern = functools.partial(_bias_kernel, q0=q0, k0=k0, nq=nq, nk=nk, strips=strips)
    return pl.pallas_call(
        kern,
        grid=(N_HEADS,),
        in_specs=[pl.BlockSpec(memory_space=pltpu.SMEM)],
        out_specs=pl.BlockSpec((1, strips * nq, nk), lambda h: (h, 0, 0)),
        out_shape=jax.ShapeDtypeStruct((N_HEADS, strips * nq, nk), F32),
        compiler_params=_cparams(("arbitrary",)),
        name="bias_table",
    )(rel_bias)


def _softmax_pv(scores, values):
    m = scores[0].max(axis=-1, keepdims=True)
    for s in scores[1:]:
        m = jnp.maximum(m, s.max(axis=-1, keepdims=True))
    acc = None
    den = None
    for s, v in zip(scores, values):
        p = jnp.exp(s - m)
        d = p.sum(axis=-1, keepdims=True)
        o = jnp.dot(p.astype(BF16), v, preferred_element_type=F32)
        acc = o if acc is None else acc + o
        den = d if den is None else den + d
    return acc / den


def _attn_prompt_kernel(q_ref, k_ref, v_ref, tab_ref, o_ref, *, tq, band):
    qi = pl.program_id(1)
    t0 = pl.multiple_of(qi * tq, tq)
    lane = lax.broadcasted_iota(I32, (1, LANES), 1)
    col = lax.broadcasted_iota(I32, (1, band), 1)
    colmask = jnp.where(col + t0 >= WIN_ROWS, 0.0, NEG_INF).astype(F32)
    for hp in range(N_HEADS // 2):
        ls = slice(hp * LANES, (hp + 1) * LANES)
        q2 = q_ref[0, :, ls]
        k2 = k_ref[0, pl.ds(t0, band), ls]
        v2 = v_ref[0, pl.ds(t0, band), ls]
        outs = []
        for a in range(2):
            qa = jnp.where((lane < HEAD_DIM) == (a == 0), q2, jnp.zeros_like(q2))
            s = lax.dot_general(qa, k2, (((1,), (1,)), ((), ())), preferred_element_type=F32)
            s = s + tab_ref[2 * hp + a] + colmask
            outs.append(_softmax_pv([s], [v2]))
        o_ref[0, :, ls] = jnp.where(lane < HEAD_DIM, outs[0], outs[1]).astype(BF16)


def _attn_prompt(q, k_pad, v_pad, table):
    b, l, _ = q.shape
    tq = Q_TILE
    band = tq + WIN_ROWS
    kern = functools.partial(_attn_prompt_kernel, tq=tq, band=band)
    return pl.pallas_call(
        kern,
        grid=(b, l // tq),
        in_specs=[pl.BlockSpec((1, tq, D_ATTN), lambda bi, i: (bi, i, 0)),
                  pl.BlockSpec((1, l + WIN_ROWS, D_ATTN), lambda bi, i: (bi, 0, 0)),
                  pl.BlockSpec((1, l + WIN_ROWS, D_ATTN), lambda bi, i: (bi, 0, 0)),
                  pl.BlockSpec(table.shape, lambda bi, i: (0, 0, 0))],
        out_specs=pl.BlockSpec((1, tq, D_ATTN), lambda bi, i: (bi, i, 0)),
        out_shape=jax.ShapeDtypeStruct((b, l, D_ATTN), BF16),
        compiler_params=_cparams(("arbitrary", "arbitrary"), VMEM_LIMIT),
        name="attn_prompt",
    )(q, k_pad, v_pad, table)


def _attn_sample_kernel(q_ref, k_ref, v_ref, ck_ref, cv_ref, tabc_ref, tabn_ref, o_ref):
    lane = lax.broadcasted_iota(I32, (1, LANES), 1)
    for hp in range(N_HEADS // 2):
        ls = slice(hp * LANES, (hp + 1) * LANES)
        q2 = q_ref[0, :, ls]
        kn = k_ref[0, :, ls]
        vn = v_ref[0, :, ls]
        kc = ck_ref[0, :, ls].astype(BF16)
        vc = cv_ref[0, :, ls].astype(BF16)
        outs = []
        for a in range(2):
            h = 2 * hp + a
            qa = jnp.where((lane < HEAD_DIM) == (a == 0), q2, jnp.zeros_like(q2))
            sc = lax.dot_general(qa, kc, (((1,), (1,)), ((), ())), preferred_element_type=F32) + tabc_ref[h]
            sn = lax.dot_general(qa, kn, (((1,), (1,)), ((), ())), preferred_element_type=F32) + tabn_ref[h]
            outs.append(_softmax_pv([sc, sn], [vc, vn]))
        o_ref[0, :, ls] = jnp.where(lane < HEAD_DIM, outs[0], outs[1]).astype(BF16)


def _attn_sample(q, k, v, cache_k, cache_v, tabc, tabn):
    b, s, _ = q.shape
    nc = cache_k.shape[1]
    return pl.pallas_call(
        _attn_sample_kernel,
        grid=(b,),
        in_specs=[pl.BlockSpec((1, s, D_ATTN), lambda bi: (bi, 0, 0)),
                  pl.BlockSpec((1, s, D_ATTN), lambda bi: (bi, 0, 0)),
                  pl.BlockSpec((1, s, D_ATTN), lambda bi: (bi, 0, 0)),
                  pl.BlockSpec((1, nc, D_ATTN), lambda bi: (bi, 0, 0)),
                  pl.BlockSpec((1, nc, D_ATTN), lambda bi: (bi, 0, 0)),
                  pl.BlockSpec(tabc.shape, lambda bi: (0, 0, 0)),
                  pl.BlockSpec(tabn.shape, lambda bi: (0, 0, 0))],
        out_specs=pl.BlockSpec((1, s, D_ATTN), lambda bi: (bi, 0, 0)),
        out_shape=jax.ShapeDtypeStruct((b, s, D_ATTN), BF16),
        compiler_params=_cparams(("arbitrary",), VMEM_LIMIT),
        name="attn_sample",
    )(q, k, v, cache_k, cache_v, tabc, tabn)


def _s5_prep_kernel(are_ref, aim_ref, ldt_ref, bre_ref, bim_ref, abre_ref, abim_ref, bbre_ref, bbim_ref):
    a_re = are_ref[...]
    a_im = aim_ref[...]
    dt = jnp.exp(ldt_ref[...])
    mag = jnp.exp(dt * a_re)
    abar_re = mag * jnp.cos(dt * a_im)
    abar_im = mag * jnp.sin(dt * a_im)
    den = a_re * a_re + a_im * a_im
    num_re = abar_re - 1.0
    coef_re = (num_re * a_re + abar_im * a_im) / den
    coef_im = (abar_im * a_re - num_re * a_im) / den
    abre_ref[...] = abar_re
    abim_ref[...] = abar_im
    b_re = bre_ref[...]
    b_im = bim_ref[...]
    bbre_ref[...] = coef_re * b_re - coef_im * b_im
    bbim_ref[...] = coef_re * b_im + coef_im * b_re


def _s5_prep(a_re, a_im, log_dt, b_re, b_im):
    g, p = a_re.shape
    bt_re = jnp.transpose(b_re, (0, 2, 1))
    bt_im = jnp.transpose(b_im, (0, 2, 1))
    i = bt_re.shape[1]
    sds = jax.ShapeDtypeStruct
    return pl.pallas_call(
        _s5_prep_kernel,
        out_shape=[sds((g, 1, p), F32), sds((g, 1, p), F32), sds((g, i, p), F32), sds((g, i, p), F32)],
        name="s5_discretise",
    )(a_re.reshape(g, 1, p), a_im.reshape(g, 1, p), log_dt.reshape(g, 1, 1), bt_re, bt_im)


def _s5_scan_kernel(u_ref, bb_ref, cc_ref, d_ref, are_ref, aim_ref, s0_ref, y_ref, sf_ref, bu_ref, st_ref,
                    *, nb, tc):
    t_step = pl.program_id(1)
    half = bu_ref.shape[1] // 2

    @pl.when(t_step == 0)
    def _():
        st_ref[...] = s0_ref[0]

    u = u_ref[...]
    bu_ref[...] = jnp.dot(u, bb_ref[0], preferred_element_type=F32)
    a_re = jnp.broadcast_to(are_ref[0], (nb, half))
    a_im = jnp.broadcast_to(aim_ref[0], (nb, half))

    def step(t, carry):
        x_re, x_im = carry
        rows = pl.ds(pl.multiple_of(t * nb, nb), nb)
        n_re = a_re * x_re - a_im * x_im + bu_ref[rows, 0:half]
        n_im = a_re * x_im + a_im * x_re + bu_ref[rows, half:2 * half]
        bu_ref[rows, 0:half] = n_re
        bu_ref[rows, half:2 * half] = n_im
        return n_re, n_im

    x_re, x_im = lax.fori_loop(0, tc, step, (st_ref[:, 0:half], st_ref[:, half:2 * half]), unroll=2)
    st_ref[:, 0:half] = x_re
    st_ref[:, half:2 * half] = x_im
    sf_ref[0] = st_ref[...]
    y = jnp.dot(bu_ref[...].astype(BF16), cc_ref[0], preferred_element_type=F32)
    y = y + d_ref[0] * u.astype(F32)
    y_ref[...] = jax.nn.gelu(y).astype(BF16)


def _s5_scan(u_tm, bb, cc, dvec, abar_re, abar_im, s0, *, nb, l):
    d_ssm = u_tm.shape[1]
    npk = bb.shape[0]
    cin = bb.shape[1]
    two_half = bb.shape[2]
    tc = min(SSM_TIME, l)
    rows = tc * nb
    kern = functools.partial(_s5_scan_kernel, nb=nb, tc=tc)
    return pl.pallas_call(
        kern,
        grid=(npk, l // tc),
        in_specs=[pl.BlockSpec((rows, cin), lambda p, t: (t, p)),
                  pl.BlockSpec((1, cin, two_half), lambda p, t: (p, 0, 0)),
                  pl.BlockSpec((1, two_half, cin), lambda p, t: (p, 0, 0)),
                  pl.BlockSpec((1, 1, cin), lambda p, t: (p, 0, 0)),
                  pl.BlockSpec((1, 1, two_half // 2), lambda p, t: (p, 0, 0)),
                  pl.BlockSpec((1, 1, two_half // 2), lambda p, t: (p, 0, 0)),
                  pl.BlockSpec((1, nb, two_half), lambda p, t: (p, 0, 0))],
        out_specs=[pl.BlockSpec((rows, cin), lambda p, t: (t, p)),
                   pl.BlockSpec((1, nb, two_half), lambda p, t: (p, 0, 0))],
        out_shape=[jax.ShapeDtypeStruct((l * nb, d_ssm), BF16),
                   jax.ShapeDtypeStruct((npk, nb, two_half), F32)],
        scratch_shapes=[pltpu.VMEM((rows, two_half), F32), pltpu.VMEM((nb, two_half), F32)],
        compiler_params=_cparams(("arbitrary", "arbitrary"), VMEM_LIMIT),
        name="s5_scan",
    )(u_tm, bb, cc, dvec, abar_re, abar_im, s0)


def _s5_pack_weights(bb_re, bb_im, c_re, c_im, d, abar_re, abar_im):
    g, i, p = bb_re.shape
    npk = g // SSM_PACK
    eye = jnp.eye(SSM_PACK, dtype=F32)

    def blockdiag_in(m):
        m = m.reshape(npk, SSM_PACK, i, p)
        return jnp.einsum("ngip,gh->ngihp", m, eye).reshape(npk, SSM_PACK * i, SSM_PACK * p)

    def blockdiag_out(m):
        m = m.reshape(npk, SSM_PACK, i, p)
        return jnp.einsum("ngip,gh->ngphi", m, eye).reshape(npk, SSM_PACK * p, SSM_PACK * i)

    bb = jnp.concatenate([blockdiag_in(bb_re), blockdiag_in(bb_im)], axis=2).astype(BF16)
    cc = jnp.concatenate([blockdiag_out(c_re), -blockdiag_out(c_im)], axis=1).astype(BF16)
    dvec = d.reshape(npk, 1, SSM_PACK * i)
    are = abar_re.reshape(npk, 1, SSM_PACK * p)
    aim = abar_im.reshape(npk, 1, SSM_PACK * p)
    return bb, cc, dvec, are, aim


def _state_to_packs(s_re, s_im):
    b, g, p = s_re.shape
    npk = g // SSM_PACK
    re = jnp.transpose(s_re.reshape(b, npk, SSM_PACK * p), (1, 0, 2))
    im = jnp.transpose(s_im.reshape(b, npk, SSM_PACK * p), (1, 0, 2))
    return jnp.concatenate([re, im], axis=2)


def _packs_to_state(sf, g, p):
    npk, b, two_half = sf.shape
    half = two_half // 2
    re = jnp.transpose(sf[:, :, :half], (1, 0, 2)).reshape(b, g, p)
    im = jnp.transpose(sf[:, :, half:], (1, 0, 2)).reshape(b, g, p)
    return re, im


def _outproj_kernel(attn_ref, ssm_ref, x_ref, mod_ref, wglu_ref, bglu_ref, wout_ref, gpm_ref, gpf_ref,
                    wr_ref, br_ref, x1_ref, h2_ref, lg_ref):
    y = ssm_ref[...]
    z = jnp.dot(y, wglu_ref[...], preferred_element_type=F32) + bglu_ref[...]
    s = (y.astype(F32) * jax.nn.sigmoid(z)).astype(BF16)
    m = jnp.dot(attn_ref[0], wout_ref[0:D_ATTN, :], preferred_element_type=F32)
    m = m + jnp.dot(s, wout_ref[D_ATTN:, :], preferred_element_type=F32)
    x1 = x_ref[0] + mod_ref[0, 2:3, :] * _rms(m, gpm_ref[...])
    x1_ref[0] = x1
    h2 = (_rms(x1, gpf_ref[...]) * (1.0 + mod_ref[0, 4:5, :]) + mod_ref[0, 3:4, :]).astype(BF16)
    h2_ref[...] = h2
    lg_ref[...] = jnp.dot(h2, wr_ref[...], preferred_element_type=F32) + br_ref[...]


def _outproj(attn, ssm2d, x, mod, w_glu_bf, b_glu, w_out_bf, g_post_mix, g_pre_ffn, w_router_bf, b_router):
    b, l, d = x.shape
    tm = min(ROW_TILE, l)
    nt = l // tm
    d_ssm = w_glu_bf.shape[0]
    ne = w_router_bf.shape[1]
    c2 = lambda bi, j: (0, 0)
    return pl.pallas_call(
        _outproj_kernel,
        grid=(b, nt),
        in_specs=[pl.BlockSpec((1, tm, D_ATTN), lambda bi, j: (bi, j, 0)),
                  pl.BlockSpec((tm, d_ssm), lambda bi, j: (j, bi)),
                  pl.BlockSpec((1, tm, d), lambda bi, j: (bi, j, 0)),
                  pl.BlockSpec((1, 6, d), lambda bi, j: (bi, 0, 0)),
                  pl.BlockSpec(w_glu_bf.shape, c2),
                  pl.BlockSpec((1, d_ssm), c2),
                  pl.BlockSpec(w_out_bf.shape, c2),
                  pl.BlockSpec((1, d), c2),
                  pl.BlockSpec((1, d), c2),
                  pl.BlockSpec(w_router_bf.shape, c2),
                  pl.BlockSpec((1, ne), c2)],
        out_specs=[pl.BlockSpec((1, tm, d), lambda bi, j: (bi, j, 0)),
                   pl.BlockSpec((tm, d), lambda bi, j: (bi * nt + j, 0)),
                   pl.BlockSpec((tm, ne), lambda bi, j: (bi * nt + j, 0))],
        out_shape=[jax.ShapeDtypeStruct((b, l, d), F32),
                   jax.ShapeDtypeStruct((b * l, d), BF16),
                   jax.ShapeDtypeStruct((b * l, ne), F32)],
        compiler_params=_cparams(("arbitrary", "arbitrary"), VMEM_LIMIT),
        name="glu_outproj_norms_router",
    )(attn, ssm2d, x, mod, w_glu_bf, b_glu.reshape(1, d_ssm), w_out_bf, g_post_mix.reshape(1, d),
      g_pre_ffn.reshape(1, d), w_router_bf, b_router.reshape(1, ne))


def _router_kernel(lg_ref, tri_ref, info_ref, pc_ref):
    lg = lg_ref[...]
    ne, tb = lg.shape
    eidx = lax.broadcasted_iota(I32, (ne, tb), 0).astype(F32)
    work = lg
    sels, vals = [], []
    for _ in range(TOP_K):
        m = work.max(axis=0, keepdims=True)
        idx = jnp.where(work == m, eidx, float(ne)).min(axis=0, keepdims=True)
        sel = eidx == idx
        sels.append(sel)
        vals.append(m)
        work = jnp.where(sel, -jnp.inf, work)
    ex = [jnp.exp(v - vals[0]) for v in vals]
    den = ex[0] + ex[1] + ex[2] + ex[3]
    onehot = (sels[0] | sels[1] | sels[2] | sels[3]).astype(F32)
    cnt = onehot.sum(axis=1, keepdims=True).astype(I32)
    pc = ((cnt + (BF16_ROWS - 1)) >> 4) << 4
    pc_ref[0] = pc
    rank = jnp.dot(onehot.astype(BF16), tri_ref[...], preferred_element_type=F32)
    low = jnp.where(lax.broadcasted_iota(I32, (ne, LANES), 1) < lax.broadcasted_iota(I32, (ne, LANES), 0), 1.0, 0.0)
    pcb = jnp.concatenate([jnp.broadcast_to(pc.astype(F32), (ne, LANES)), jnp.zeros((LANES - ne, LANES), F32)], axis=0)
    off = jnp.dot(low.astype(BF16), pcb.astype(BF16), preferred_element_type=F32)[:, 0:1]
    posmat = rank + off
    for k in range(TOP_K):
        info_ref[k:k + 1, :] = jnp.where(sels[k], posmat, 0.0).sum(axis=0, keepdims=True)
        info_ref[TOP_K + k:TOP_K + k + 1, :] = ex[k] / den


def _router(logits_t, tb):
    ne, t = logits_t.shape
    nb = t // tb
    tri = jnp.triu(jnp.ones((tb, tb), BF16), k=1)
    return pl.pallas_call(
        _router_kernel,
        grid=(nb,),
        in_specs=[pl.BlockSpec((ne, tb), lambda i: (0, i)),
                  pl.BlockSpec((tb, tb), lambda i: (0, 0))],
        out_specs=[pl.BlockSpec((2 * TOP_K, tb), lambda i: (0, i)),
                   pl.BlockSpec((1, ne, 1), lambda i: (i, 0, 0))],
        out_shape=[jax.ShapeDtypeStruct((2 * TOP_K, t), F32),
                   jax.ShapeDtypeStruct((nb, ne, 1), I32)],
        compiler_params=_cparams(("arbitrary",)),
        name="router_top4",
    )(logits_t, tri)


def _local_chunks(tb, ne):
    rows = TOP_K * tb + ne * (BF16_ROWS - 1)
    return -(-rows // SORT_ROWS) * (SORT_ROWS // BF16_ROWS)


def _sort_plan(pc, tb):
    nb, ne = pc.shape
    nch = _local_chunks(tb, ne)
    n_tiles = (TOP_K * tb * nb + nb * ne * (BF16_ROWS - 1)) // FFN_TILE + ne
    loc_end = jnp.cumsum(pc, axis=1)
    loc_off = loc_end - pc
    tot = pc.sum(axis=0)
    tot_pad = ((tot + FFN_TILE - 1) // FFN_TILE) * FFN_TILE
    base_end = jnp.cumsum(tot_pad)
    base = base_end - tot_pad
    glob = base[None, :] + jnp.cumsum(pc, axis=0) - pc
    n_used = (base_end[-1] // FFN_TILE).astype(I32)
    tile_row = jnp.arange(n_tiles, dtype=I32) * FFN_TILE
    tile_expert = jnp.minimum((base_end[None, :] <= tile_row[:, None]).sum(axis=1), ne - 1).astype(I32)
    r = jnp.arange(nch, dtype=I32) * BF16_ROWS
    e_of = jnp.minimum((loc_end[:, None, :] <= r[None, :, None]).sum(axis=2), ne - 1)
    shift = glob - loc_off
    owner = e_of[:, :, None] == jnp.arange(ne, dtype=I32)[None, None, :]
    dst = jnp.where(owner, shift[:, None, :], 0).sum(axis=2) + r[None, :]
    chunk_dst = (dst // BF16_ROWS).astype(I32)
    n_valid = (loc_end[:, -1] // BF16_ROWS).astype(I32)
    tail_n = ((tot_pad - tot) // BF16_ROWS).astype(I32)
    tail_dst = ((base + tot) // BF16_ROWS).astype(I32)
    return dict(nch=nch, n_tiles=n_tiles, chunk_dst=chunk_dst.reshape(-1), n_valid=n_valid, n_used=n_used,
                tile_expert=tile_expert, tail_n=tail_n, tail_dst=tail_dst)


def _chunk_copy(buf, hbm, slot, c, dst_chunk, sem, to_hbm):
    v = buf.at[slot, pl.ds(pl.multiple_of(c * BF16_ROWS, BF16_ROWS), BF16_ROWS)]
    h = hbm.at[pl.ds(pl.multiple_of(dst_chunk * BF16_ROWS, BF16_ROWS), BF16_ROWS)]
    return pltpu.make_async_copy(v, h, sem) if to_hbm else pltpu.make_async_copy(h, v, sem)


def _local_row_ids(rt, tb):
    return lax.broadcasted_iota(I32, (rt, tb), 0).astype(F32).astype(BF16)


def _local_pos(pos_row, r0, rt):
    return jnp.clip(pos_row - float(r0), -1.0, float(rt)).astype(BF16)


def _dispatch_kernel(cd_ref, nv_ref, tn_ref, td_ref, nu_ref, h_ref, info_ref, xs_ref, buf_ref, zero_ref, ztile_ref,
                     sem, zsem, *, nch, rt, n_tiles):
    b = pl.program_id(0)
    nb = pl.num_programs(0)
    slot = b % 2
    tb = h_ref.shape[0]
    h = h_ref[...]
    rid = _local_row_ids(rt, tb)
    one = jnp.ones((rt, tb), BF16)
    for r0 in range(0, nch * BF16_ROWS, rt):
        p = jnp.zeros((rt, tb), BF16)
        for k in range(TOP_K):
            p = jnp.where(rid == _local_pos(info_ref[k:k + 1, :], r0, rt), one, p)
        buf_ref[slot, r0:r0 + rt, :] = jnp.dot(p, h, preferred_element_type=F32).astype(BF16)

    def issue(c, _):
        _chunk_copy(buf_ref, xs_ref, slot, c, cd_ref[b * nch + c], sem.at[slot], True).start()
        return 0

    lax.fori_loop(0, nv_ref[b], issue, 0)

    def drain(step, sl):
        def wait(c, _):
            _chunk_copy(buf_ref, xs_ref, sl, c, cd_ref[step * nch + c], sem.at[sl], True).wait()
            return 0

        lax.fori_loop(0, nv_ref[step], wait, 0)

    @pl.when(b > 0)
    def _():
        drain(b - 1, 1 - slot)

    @pl.when(b == nb - 1)
    def _():
        drain(b, slot)
        zero_ref[...] = jnp.zeros_like(zero_ref)
        for e in range(N_EXPERTS):
            def zcopy(c, e=e):
                dst = pl.multiple_of((td_ref[e] + c) * BF16_ROWS, BF16_ROWS)
                return pltpu.make_async_copy(zero_ref, xs_ref.at[pl.ds(dst, BF16_ROWS)], zsem)

            lax.fori_loop(0, tn_ref[e], lambda c, _: (zcopy(c).start(), 0)[1], 0)
            lax.fori_loop(0, tn_ref[e], lambda c, _: (zcopy(c).wait(), 0)[1], 0)

        ztile_ref[...] = jnp.zeros_like(ztile_ref)

        def ztile(i):
            dst = pl.multiple_of(i * FFN_TILE, FFN_TILE)
            return pltpu.make_async_copy(ztile_ref, xs_ref.at[pl.ds(dst, FFN_TILE)], zsem)

        lax.fori_loop(nu_ref[0], n_tiles, lambda i, _: (ztile(i).start(), 0)[1], 0)
        lax.fori_loop(nu_ref[0], n_tiles, lambda i, _: (ztile(i).wait(), 0)[1], 0)


def _dispatch(h2, info, plan, tb):
    t, d = h2.shape
    nb = t // tb
    nch = plan["nch"]
    rows = plan["n_tiles"] * FFN_TILE
    assert (nch * BF16_ROWS) % SORT_ROWS == 0
    kern = functools.partial(_dispatch_kernel, nch=nch, rt=SORT_ROWS, n_tiles=plan["n_tiles"])
    grid_spec = pltpu.PrefetchScalarGridSpec(
        num_scalar_prefetch=5,
        grid=(nb,),
        in_specs=[pl.BlockSpec((tb, d), lambda i, *_: (i, 0)),
                  pl.BlockSpec((2 * TOP_K, tb), lambda i, *_: (0, i))],
        out_specs=pl.BlockSpec(memory_space=pl.ANY),
        scratch_shapes=[pltpu.VMEM((2, nch * BF16_ROWS, d), BF16),
                        pltpu.VMEM((BF16_ROWS, d), BF16),
                        pltpu.VMEM((FFN_TILE, d), BF16),
                        pltpu.SemaphoreType.DMA((2,)),
                        pltpu.SemaphoreType.DMA],
    )
    return pl.pallas_call(
        kern,
        grid_spec=grid_spec,
        out_shape=jax.ShapeDtypeStruct((rows, d), BF16),
        compiler_params=_cparams(("arbitrary",), VMEM_LIMIT),
        name="moe_dispatch",
    )(plan["chunk_dst"], plan["n_valid"], plan["tail_n"], plan["tail_dst"], plan["n_used"].reshape(1), h2, info)


def _ffn_kernel(te_ref, nu_ref, x_ref, wgu_ref, bgu_ref, wd_ref, bd_ref, perm_ref, y_ref, wgu_s, wd_s):
    i = pl.program_id(0)
    f = wd_s.shape[0]
    prev = te_ref[jnp.maximum(i - 1, 0)]
    new_expert = (i == 0) | (te_ref[i] != prev)

    @pl.when((i < nu_ref[0]) & new_expert)
    def _():
        for c in range(0, 2 * f, 2 * LANES):
            blk = wgu_ref[0, :, c:c + 2 * LANES].astype(BF16)
            wgu_s[:, c:c + 2 * LANES] = jnp.dot(blk, perm_ref[...], preferred_element_type=F32).astype(BF16)
        wd_s[...] = wd_ref[0].astype(BF16)

    @pl.when(i < nu_ref[0])
    def _():
        x = x_ref[...]
        acts = []
        for c in range(0, 2 * f, 2 * LANES):
            gu = jnp.dot(x, wgu_s[:, c:c + 2 * LANES], preferred_element_type=F32) + bgu_ref[0, :, c:c + 2 * LANES]
            glu = jnp.minimum(gu[:, :LANES], SWIGLU_LIMIT)
            lin = jnp.clip(gu[:, LANES:], -SWIGLU_LIMIT, SWIGLU_LIMIT)
            acts.append((glu * jax.nn.sigmoid(SWIGLU_ALPHA * glu) * (lin + 1.0)).astype(BF16))
        act = jnp.concatenate(acts, axis=1)
        y_ref[...] = (jnp.dot(act, wd_s[...], preferred_element_type=F32) + bd_ref[0]).astype(BF16)

    @pl.when(i >= nu_ref[0])
    def _():
        y_ref[...] = jnp.zeros_like(y_ref)


def _ffn(xs, plan, w_gate_up, b_gu_perm, w_down, b_down):
    rows, d = xs.shape
    ne, _, f2 = w_gate_up.shape
    f = f2 // 2
    n_tiles = plan["n_tiles"]
    src = jnp.concatenate([jnp.arange(LANES) * 2, jnp.arange(LANES) * 2 + 1])
    perm = (jnp.arange(2 * LANES)[:, None] == src[None, :]).astype(BF16)

    def xi(i, te, nu):
        return (jnp.minimum(i, nu[0] - 1), 0)

    def wi(i, te, nu):
        return (te[jnp.minimum(i, nu[0] - 1)], 0, 0)

    grid_spec = pltpu.PrefetchScalarGridSpec(
        num_scalar_prefetch=2,
        grid=(n_tiles,),
        in_specs=[pl.BlockSpec((FFN_TILE, d), xi),
                  pl.BlockSpec((1, d, f2), wi),
                  pl.BlockSpec((1, 1, f2), wi),
                  pl.BlockSpec((1, f, d), wi),
                  pl.BlockSpec((1, 1, d), wi),
                  pl.BlockSpec((2 * LANES, 2 * LANES), lambda i, te, nu: (0, 0))],
        out_specs=pl.BlockSpec((FFN_TILE, d), lambda i, te, nu: (i, 0)),
        scratch_shapes=[pltpu.VMEM((d, f2), BF16), pltpu.VMEM((f, d), BF16)],
    )
    return pl.pallas_call(
        _ffn_kernel,
        grid_spec=grid_spec,
        out_shape=jax.ShapeDtypeStruct((rows, d), BF16),
        compiler_params=_cparams(("arbitrary",), VMEM_LIMIT),
        name="moe_expert_ffn",
    )(plan["tile_expert"], plan["n_used"].reshape(1), xs, w_gate_up, b_gu_perm.reshape(ne, 1, f2), w_down,
      b_down.reshape(ne, 1, d), perm)


def _combine_kernel(cd_ref, nv_ref, ys_ref, info_ref, x1_ref, mod_ref, g_ref, o_ref, buf_ref, w_ref, sem,
                    *, nch, rt):
    b = pl.program_id(0)
    tb = x1_ref.shape[0]
    nbat = mod_ref.shape[0]
    rb = tb // nbat

    @pl.when(b == 0)
    def _():
        buf_ref[...] = jnp.zeros_like(buf_ref)

    def fetch(c, _):
        _chunk_copy(buf_ref, ys_ref, 0, c, cd_ref[b * nch + c], sem, False).start()
        return 0

    lax.fori_loop(0, nv_ref[b], fetch, 0)

    rid = _local_row_ids(rt, tb)
    for r0 in range(0, nch * BF16_ROWS, rt):
        w = jnp.zeros((rt, tb), BF16)
        for k in range(TOP_K):
            gate = jnp.broadcast_to(info_ref[TOP_K + k:TOP_K + k + 1, :].astype(BF16), (rt, tb))
            w = jnp.where(rid == _local_pos(info_ref[k:k + 1, :], r0, rt), gate, w)
        w_ref[r0:r0 + rt, :] = w

    def wait(c, _):
        _chunk_copy(buf_ref, ys_ref, 0, c, cd_ref[b * nch + c], sem, False).wait()
        return 0

    lax.fori_loop(0, nv_ref[b], wait, 0)

    moe = lax.dot_general(w_ref[...], buf_ref[0], (((0,), (0,)), ((), ())), preferred_element_type=F32)
    out = _rms(moe, g_ref[...])
    for s in range(nbat):
        rs = slice(s * rb, (s + 1) * rb)
        o_ref[rs, :] = x1_ref[rs, :] + mod_ref[s, 5:6, :] * out[rs, :]


def _combine(ys, info, x1, mod, g_post_ffn, plan, tb, l):
    t, d = x1.shape
    nb = t // tb
    nch = plan["nch"]
    if l >= tb:
        nbat, mi = 1, (lambda i, *_: (i // (l // tb), 0, 0))
    else:
        nbat, mi = tb // l, (lambda i, *_: (i, 0, 0))
    kern = functools.partial(_combine_kernel, nch=nch, rt=SORT_ROWS)
    grid_spec = pltpu.PrefetchScalarGridSpec(
        num_scalar_prefetch=2,
        grid=(nb,),
        in_specs=[pl.BlockSpec(memory_space=pl.ANY),
                  pl.BlockSpec((2 * TOP_K, tb), lambda i, *_: (0, i)),
                  pl.BlockSpec((tb, d), lambda i, *_: (i, 0)),
                  pl.BlockSpec((nbat, 6, d), mi),
                  pl.BlockSpec((1, d), lambda i, *_: (0, 0))],
        out_specs=pl.BlockSpec((tb, d), lambda i, *_: (i, 0)),
        scratch_shapes=[pltpu.VMEM((1, nch * BF16_ROWS, d), BF16), pltpu.VMEM((nch * BF16_ROWS, tb), BF16),
                        pltpu.SemaphoreType.DMA],
    )
    return pl.pallas_call(
        kern,
        grid_spec=grid_spec,
        out_shape=jax.ShapeDtypeStruct((t, d), F32),
        compiler_params=_cparams(("arbitrary",), VMEM_LIMIT),
        name="moe_combine",
    )(plan["chunk_dst"], plan["n_valid"], ys, info, x1, mod, g_post_ffn.reshape(1, d))


def _moe(h2, logits, x1, mod, l, g_post_ffn, w_gate_up, b_gu_perm, w_down, b_down):
    t, d = h2.shape
    tb = min(ROW_TILE, t)
    info, pc = _router(logits[:, :N_EXPERTS].T, tb)
    plan = _sort_plan(pc.reshape(t // tb, N_EXPERTS), tb)
    xs = _dispatch(h2, info, plan, tb)
    ys = _ffn(xs, plan, w_gate_up, b_gu_perm, w_down, b_down)
    return _combine(ys, info, x1, mod, g_post_ffn, plan, tb, l)


def _stream(x, mod, wts, attn_fn, s0_packs):
    b, l, d = x.shape
    q, k_att, v_att, u2d, kf, vf = _inproj(x, mod, wts["g_pre_mix"], wts["w_in"], pad_front=attn_fn is None)
    if attn_fn is None:
        attn = _attn_prompt(q, k_att, v_att, wts["table_prompt"])
    else:
        attn = attn_fn(q, k_att, v_att)
    d_ssm = u2d.shape[1] // b
    u_tm = u2d.reshape(l * b, d_ssm)
    y_tm, sf = _s5_scan(u_tm, wts["bb"], wts["cc"], wts["dvec"], wts["are"], wts["aim"], s0_packs, nb=b, l=l)
    ssm2d = y_tm.reshape(l, b * d_ssm)
    x1, h2, logits = _outproj(attn, ssm2d, x, mod, wts["w_glu"], wts["b_glu"], wts["w_out"], wts["g_post_mix"],
                              wts["g_pre_ffn"], wts["w_router"], wts["b_router"])
    y = _moe(h2, logits, x1.reshape(b * l, d), mod, l, wts["g_post_ffn"], wts["w_gate_up"], wts["b_gu_perm"],
             wts["w_down"], wts["b_down"])
    return y.reshape(b, l, d), kf, vf, sf


def kernel(x_prompt, x_sample, cache_k, cache_v, state_ssm_re, state_ssm_im, c_prompt, c_sample, w_ada, b_ada, g_pre_mix, g_post_mix, g_pre_ffn, g_post_ffn, w_in, rel_bias, ssm_a_re, ssm_a_im, ssm_log_dt, ssm_b_re, ssm_b_im, ssm_c_re, ssm_c_im, ssm_d, w_glu, b_glu, w_out, w_router, b_router, w_gate_up, b_gate_up, w_down, b_down):
    depth = w_ada.shape[0]
    assert depth == 1, "single-layer trunk"
    bp, lp, d = x_prompt.shape
    bs, ls, _ = x_sample.shape
    n_cache = cache_k.shape[2]
    past_len = PAST_LEN
    g, p = ssm_a_re.shape[1:]
    lyr = 0

    mod = _ada(jnp.concatenate([c_prompt, c_sample], axis=0), w_ada[lyr], b_ada[lyr])
    mod = mod.reshape(bp + bs, 6, d)

    abre, abim, bbre, bbim = _s5_prep(ssm_a_re[lyr], ssm_a_im[lyr], ssm_log_dt[lyr], ssm_b_re[lyr], ssm_b_im[lyr])
    bb, cc, dvec, are, aim = _s5_pack_weights(bbre, bbim, ssm_c_re[lyr], ssm_c_im[lyr], ssm_d[lyr], abre, abim)

    f2 = w_gate_up.shape[-1]
    b_gu_perm = b_gate_up[lyr].reshape(N_EXPERTS, f2 // (2 * LANES), LANES, 2)
    b_gu_perm = jnp.transpose(b_gu_perm, (0, 1, 3, 2)).reshape(N_EXPERTS, f2)

    rb = rel_bias[lyr]
    wts = dict(
        g_pre_mix=g_pre_mix[lyr], g_post_mix=g_post_mix[lyr], g_pre_ffn=g_pre_ffn[lyr], g_post_ffn=g_post_ffn[lyr],
        w_in=w_in[lyr].astype(BF16), w_glu=w_glu[lyr].astype(BF16), b_glu=b_glu[lyr], w_out=w_out[lyr].astype(BF16),
        w_router=jnp.pad(w_router[lyr], ((0, 0), (0, LANES - N_EXPERTS))).astype(BF16),
        b_router=jnp.pad(b_router[lyr], (0, LANES - N_EXPERTS)),
        w_gate_up=w_gate_up[lyr], b_gu_perm=b_gu_perm, w_down=w_down[lyr], b_down=b_down[lyr],
        bb=bb, cc=cc, dvec=dvec, are=are, aim=aim,
        table_prompt=_bias_table(rb, q0=WIN_ROWS, k0=0, nq=CHUNK, nk=Q_TILE + WIN_ROWS, strips=Q_TILE // CHUNK),
    )

    zero_state = jnp.zeros((g // SSM_PACK, bp, 2 * SSM_PACK * p), F32)
    y_p, kf_p, vf_p, sf_p = _stream(x_prompt, mod[:bp], wts, None, zero_state)

    tabc = _bias_table(rb, q0=past_len, k0=past_len - n_cache, nq=ls, nk=n_cache)
    tabn = _bias_table(rb, q0=past_len, k0=past_len, nq=ls, nk=ls)
    ck = cache_k[lyr].reshape(bs, n_cache, D_ATTN)
    cv = cache_v[lyr].reshape(bs, n_cache, D_ATTN)
    attn_s = lambda q, k, v: _attn_sample(q, k, v, ck, cv, tabc, tabn)
    s0 = _state_to_packs(state_ssm_re[lyr], state_ssm_im[lyr])
    y_s, kf_s, vf_s, sf_s = _stream(x_sample, mod[bp:], wts, attn_s, s0)

    def heads(a):
        return a.reshape(1, a.shape[0], a.shape[1], N_HEADS, HEAD_DIM)

    sre_p, sim_p = _packs_to_state(sf_p, g, p)
    sre_s, sim_s = _packs_to_state(sf_s, g, p)
    return (y_p, y_s, heads(kf_p), heads(vf_p), sre_p[None], sim_p[None],
            heads(kf_s), heads(vf_s), sre_s[None], sim_s[None])
```

```python
import functools
import math

import jax
import jax.numpy as jnp
from jax import lax
from jax.experimental import pallas as pl
from jax.experimental.pallas import tpu as pltpu

F32 = jnp.float32
BF16 = jnp.bfloat16
I32 = jnp.int32
HIGHEST = lax.Precision.HIGHEST

CHUNK = 64
CHUNK_LOG2 = 6
N_PAST_CHUNKS = 8
PAST_LEN = 2048
WIN_ROWS = N_PAST_CHUNKS * CHUNK
MAX_REL = 64
N_REL = 2 * MAX_REL + 1
N_HEADS = 8
HEAD_DIM = 64
D_ATTN = N_HEADS * HEAD_DIM
SSM_GROUP = 16
SSM_STATE = 64
N_EXPERTS = 32
TOP_K = 4
SWIGLU_LIMIT = 7.0
SWIGLU_ALPHA = 1.702
RMS_EPS = 1e-6
NEG_INF = -1e30

LANES = 128
BF16_ROWS = 16
VMEM_LIMIT = 56 * 1024 * 1024

ROW_TILE = 512
Q_TILE = 256
SSM_PACK = 8
SSM_TIME = 64
FFN_TILE = 512
SORT_ROWS = 256


def _cparams(sem, vmem=None):
    return pltpu.CompilerParams(dimension_semantics=sem, vmem_limit_bytes=vmem)


def _rms(x, g):
    return x * lax.rsqrt(jnp.mean(x * x, axis=-1, keepdims=True) + RMS_EPS) * g


def _ada_kernel(c_ref, w_ref, b_ref, o_ref):
    c = c_ref[...]
    s = c * jax.nn.sigmoid(c)
    o_ref[...] = jnp.dot(s, w_ref[...], precision=HIGHEST, preferred_element_type=F32) + b_ref[...]


def _ada(c, w_ada, b_ada):
    n, d = c.shape
    nout = w_ada.shape[1]
    return pl.pallas_call(
        _ada_kernel,
        grid=(nout // d,),
        in_specs=[pl.BlockSpec((n, d), lambda j: (0, 0)),
                  pl.BlockSpec((d, d), lambda j: (0, j)),
                  pl.BlockSpec((1, d), lambda j: (0, j))],
        out_specs=pl.BlockSpec((n, d), lambda j: (0, j)),
        out_shape=jax.ShapeDtypeStruct((n, nout), F32),
        compiler_params=_cparams(("arbitrary",), VMEM_LIMIT),
        name="ada_mod",
    )(c, w_ada, b_ada.reshape(1, nout))


def _inproj_kernel(x_ref, mod_ref, g_ref, w_ref, q_ref, k_ref, v_ref, u_ref, kf_ref, vf_ref, *, first_keep, pad):
    j = pl.program_id(1)

    def compute():
        x = x_ref[0]
        h = _rms(x, g_ref[...]) * (1.0 + mod_ref[0, 1:2, :]) + mod_ref[0, 0:1, :]
        proj = jnp.dot(h.astype(BF16), w_ref[...], preferred_element_type=F32)
        q_ref[0] = (proj[:, :D_ATTN] * (HEAD_DIM ** -0.5)).astype(BF16)
        k_ref[0] = proj[:, D_ATTN:2 * D_ATTN].astype(BF16)
        v_ref[0] = proj[:, 2 * D_ATTN:3 * D_ATTN].astype(BF16)
        u_ref[...] = proj[:, 3 * D_ATTN:].astype(BF16)

        @pl.when(j >= first_keep + pad)
        def _():
            kf_ref[0] = proj[:, D_ATTN:2 * D_ATTN]
            vf_ref[0] = proj[:, 2 * D_ATTN:3 * D_ATTN]

    if pad:
        @pl.when(j == 0)
        def _():
            k_ref[...] = jnp.zeros_like(k_ref)
            v_ref[...] = jnp.zeros_like(v_ref)

        pl.when(j > 0)(compute)
    else:
        compute()


def _inproj(x, mod, g, w_in_bf, *, pad_front):
    b, l, d = x.shape
    tm = min(ROW_TILE, l)
    nt = l // tm
    keep = min(WIN_ROWS, l)
    first_keep = (l - keep) // tm
    pad = 1 if pad_front else 0
    d_ssm = w_in_bf.shape[1] - 3 * D_ATTN
    assert l % tm == 0 and keep % tm == 0 and (not pad_front or tm == WIN_ROWS)

    def xi(bi, j):
        return (bi, jnp.maximum(j - pad, 0), 0)

    def kfi(bi, j):
        return (bi, jnp.maximum(j - pad - first_keep, 0), 0)

    kern = functools.partial(_inproj_kernel, first_keep=first_keep, pad=pad)
    return pl.pallas_call(
        kern,
        grid=(b, nt + pad),
        in_specs=[pl.BlockSpec((1, tm, d), xi),
                  pl.BlockSpec((1, 6, d), lambda bi, j: (bi, 0, 0)),
                  pl.BlockSpec((1, d), lambda bi, j: (0, 0)),
                  pl.BlockSpec(w_in_bf.shape, lambda bi, j: (0, 0))],
        out_specs=[pl.BlockSpec((1, tm, D_ATTN), xi),
                   pl.BlockSpec((1, tm, D_ATTN), lambda bi, j: (bi, j, 0)),
                   pl.BlockSpec((1, tm, D_ATTN), lambda bi, j: (bi, j, 0)),
                   pl.BlockSpec((tm, d_ssm), lambda bi, j: (jnp.maximum(j - pad, 0), bi)),
                   pl.BlockSpec((1, tm, D_ATTN), kfi),
                   pl.BlockSpec((1, tm, D_ATTN), kfi)],
        out_shape=[jax.ShapeDtypeStruct((b, l, D_ATTN), BF16),
                   jax.ShapeDtypeStruct((b, l + pad * tm, D_ATTN), BF16),
                   jax.ShapeDtypeStruct((b, l + pad * tm, D_ATTN), BF16),
                   jax.ShapeDtypeStruct((l, b * d_ssm), BF16),
                   jax.ShapeDtypeStruct((b, keep, D_ATTN), F32),
                   jax.ShapeDtypeStruct((b, keep, D_ATTN), F32)],
        compiler_params=_cparams(("arbitrary", "arbitrary"), VMEM_LIMIT),
        name="prenorm_inproj",
    )(x, mod, g.reshape(1, d), w_in_bf)


def _bias_kernel(rb_ref, o_ref, *, q0, k0, nq, nk, strips):
    h = pl.program_id(0)
    qpos = q0 + lax.broadcasted_iota(I32, (nq, nk), 0)
    kpos = k0 + lax.broadcasted_iota(I32, (nq, nk), 1)
    rel = jnp.clip(kpos - qpos, -MAX_REL, MAX_REL) + MAX_REL
    qc = qpos >> CHUNK_LOG2
    kc = kpos >> CHUNK_LOG2
    allowed = (kc <= qc) & (kc >= qc - N_PAST_CHUNKS)

    def body(r, acc):
        return jnp.where(rel == r, rb_ref[r, h], acc)

    val = lax.fori_loop(0, N_REL, body, jnp.zeros((nq, nk), F32))
    strip = jnp.where(allowed, val, NEG_INF)
    o_ref[0, 0:nq, :] = strip
    for c in range(1, strips):
        o_ref[0, c * nq:(c + 1) * nq, :] = pltpu.roll(strip, c * CHUNK, axis=1)


def _bias_table(rel_bias, *, q0, k0, nq, nk, strips=1):
    kern = functools.partial(_bias_kernel, q0=q0, k0=k0, nq=nq, nk=nk, strips=strips)
    return pl.pallas_call(
        kern,
        grid=(N_HEADS,),
        in_specs=[pl.BlockSpec(memory_space=pltpu.SMEM)],
        out_specs=pl.BlockSpec((1, strips * nq, nk), lambda h: (h, 0, 0)),
        out_shape=jax.ShapeDtypeStruct((N_HEADS, strips * nq, nk), F32),
        compiler_params=_cparams(("arbitrary",)),
        name="bias_table",
    )(rel_bias)


def _softmax_pv(scores, values):
    m = scores[0].max(axis=-1, keepdims=True)
    for s in scores[1:]:
        m = jnp.maximum(m, s.max(axis=-1, keepdims=True))
    acc = None
    den = None
    for s, v in zip(scores, values):
        p = jnp.exp(s - m)
        d = p.sum(axis=-1, keepdims=True)
        o = jnp.dot(p.astype(BF16), v, preferred_element_type=F32)
        acc = o if acc is None else acc + o
        den = d if den is None else den + d
    return acc / den


def _attn_prompt_kernel(q_ref, k_ref, v_ref, tab_ref, o_ref, *, tq, band):
    qi = pl.program_id(1)
    t0 = pl.multiple_of(qi * tq, tq)
    lane = lax.broadcasted_iota(I32, (1, LANES), 1)
    col = lax.broadcasted_iota(I32, (1, band), 1)
    colmask = jnp.where(col + t0 >= WIN_ROWS, 0.0, NEG_INF).astype(F32)
    for hp in range(N_HEADS // 2):
        ls = slice(hp * LANES, (hp + 1) * LANES)
        q2 = q_ref[0, :, ls]
        k2 = k_ref[0, pl.ds(t0, band), ls]
        v2 = v_ref[0, pl.ds(t0, band), ls]
        outs = []
        for a in range(2):
            qa = jnp.where((lane < HEAD_DIM) == (a == 0), q2, jnp.zeros_like(q2))
            s = lax.dot_general(qa, k2, (((1,), (1,)), ((), ())), preferred_element_type=F32)
            s = s + tab_ref[2 * hp + a] + colmask
            outs.append(_softmax_pv([s], [v2]))
        o_ref[0, :, ls] = jnp.where(lane < HEAD_DIM, outs[0], outs[1]).astype(BF16)


def _attn_prompt(q, k_pad, v_pad, table):
    b, l, _ = q.shape
    tq = Q_TILE
    band = tq + WIN_ROWS
    kern = functools.partial(_attn_prompt_kernel, tq=tq, band=band)
    return pl.pallas_call(
        kern,
        grid=(b, l // tq),
        in_specs=[pl.BlockSpec((1, tq, D_ATTN), lambda bi, i: (bi, i, 0)),
                  pl.BlockSpec((1, l + WIN_ROWS, D_ATTN), lambda bi, i: (bi, 0, 0)),
                  pl.BlockSpec((1, l + WIN_ROWS, D_ATTN), lambda bi, i: (bi, 0, 0)),
                  pl.BlockSpec(table.shape, lambda bi, i: (0, 0, 0))],
        out_specs=pl.BlockSpec((1, tq, D_ATTN), lambda bi, i: (bi, i, 0)),
        out_shape=jax.ShapeDtypeStruct((b, l, D_ATTN), BF16),
        compiler_params=_cparams(("arbitrary", "arbitrary"), VMEM_LIMIT),
        name="attn_prompt",
    )(q, k_pad, v_pad, table)


def _attn_sample_kernel(q_ref, k_ref, v_ref, ck_ref, cv_ref, tabc_ref, tabn_ref, o_ref):
    lane = lax.broadcasted_iota(I32, (1, LANES), 1)
    for hp in range(N_HEADS // 2):
        ls = slice(hp * LANES, (hp + 1) * LANES)
        q2 = q_ref[0, :, ls]
        kn = k_ref[0, :, ls]
        vn = v_ref[0, :, ls]
        kc = ck_ref[0, :, ls].astype(BF16)
        vc = cv_ref[0, :, ls].astype(BF16)
        outs = []
        for a in range(2):
            h = 2 * hp + a
            qa = jnp.where((lane < HEAD_DIM) == (a == 0), q2, jnp.zeros_like(q2))
            sc = lax.dot_general(qa, kc, (((1,), (1,)), ((), ())), preferred_element_type=F32) + tabc_ref[h]
            sn = lax.dot_general(qa, kn, (((1,), (1,)), ((), ())), preferred_element_type=F32) + tabn_ref[h]
            outs.append(_softmax_pv([sc, sn], [vc, vn]))
        o_ref[0, :, ls] = jnp.where(lane < HEAD_DIM, outs[0], outs[1]).astype(BF16)


def _attn_sample(q, k, v, cache_k, cache_v, tabc, tabn):
    b, s, _ = q.shape
    nc = cache_k.shape[1]
    return pl.pallas_call(
        _attn_sample_kernel,
        grid=(b,),
        in_specs=[pl.BlockSpec((1, s, D_ATTN), lambda bi: (bi, 0, 0)),
                  pl.BlockSpec((1, s, D_ATTN), lambda bi: (bi, 0, 0)),
                  pl.BlockSpec((1, s, D_ATTN), lambda bi: (bi, 0, 0)),
                  pl.BlockSpec((1, nc, D_ATTN), lambda bi: (bi, 0, 0)),
                  pl.BlockSpec((1, nc, D_ATTN), lambda bi: (bi, 0, 0)),
                  pl.BlockSpec(tabc.shape, lambda bi: (0, 0, 0)),
                  pl.BlockSpec(tabn.shape, lambda bi: (0, 0, 0))],
        out_specs=pl.BlockSpec((1, s, D_ATTN), lambda bi: (bi, 0, 0)),
        out_shape=jax.ShapeDtypeStruct((b, s, D_ATTN), BF16),
        compiler_params=_cparams(("arbitrary",), VMEM_LIMIT),
        name="attn_sample",
    )(q, k, v, cache_k, cache_v, tabc, tabn)


def _s5_prep_kernel(are_ref, aim_ref, ldt_ref, bre_ref, bim_ref, abre_ref, abim_ref, bbre_ref, bbim_ref):
    a_re = are_ref[...]
    a_im = aim_ref[...]
    dt = jnp.exp(ldt_ref[...])
    mag = jnp.exp(dt * a_re)
    abar_re = mag * jnp.cos(dt * a_im)
    abar_im = mag * jnp.sin(dt * a_im)
    den = a_re * a_re + a_im * a_im
    num_re = abar_re - 1.0
    coef_re = (num_re * a_re + abar_im * a_im) / den
    coef_im = (abar_im * a_re - num_re * a_im) / den
    abre_ref[...] = abar_re
    abim_ref[...] = abar_im
    b_re = bre_ref[...]
    b_im = bim_ref[...]
    bbre_ref[...] = coef_re * b_re - coef_im * b_im
    bbim_ref[...] = coef_re * b_im + coef_im * b_re


def _s5_prep(a_re, a_im, log_dt, b_re, b_im):
    g, p = a_re.shape
    bt_re = jnp.transpose(b_re, (0, 2, 1))
    bt_im = jnp.transpose(b_im, (0, 2, 1))
    i = bt_re.shape[1]
    sds = jax.ShapeDtypeStruct
    return pl.pallas_call(
        _s5_prep_kernel,
        out_shape=[sds((g, 1, p), F32), sds((g, 1, p), F32), sds((g, i, p), F32), sds((g, i, p), F32)],
        name="s5_discretise",
    )(a_re.reshape(g, 1, p), a_im.reshape(g, 1, p), log_dt.reshape(g, 1, 1), bt_re, bt_im)


def _s5_scan_kernel(u_ref, bb_ref, cc_ref, d_ref, are_ref, aim_ref, s0_ref, y_ref, sf_ref, bu_ref, xb_ref, st_ref,
                    *, nb, tc, cin):
    t_step = pl.program_id(1)
    half = bu_ref.shape[2] // 2
    packed_rows = nb % BF16_ROWS == 0

    @pl.when(t_step == 0)
    def _():
        st_ref[...] = s0_ref[...]

    us = []
    for pk in range(2):
        u = u_ref[:, pk * cin:(pk + 1) * cin]
        us.append(u)
        bu_ref[pk] = jnp.dot(u, bb_ref[pk], preferred_element_type=F32)
    for pk in range(2):
        a_re = jnp.broadcast_to(are_ref[pk], (nb, half))
        a_im = jnp.broadcast_to(aim_ref[pk], (nb, half))
        x_re = st_ref[pk, :, 0:half]
        x_im = st_ref[pk, :, half:2 * half]
        for t in range(tc):
            rows = slice(t * nb, (t + 1) * nb)
            n_re = a_re * x_re - a_im * x_im + bu_ref[pk, rows, 0:half]
            n_im = a_re * x_im + a_im * x_re + bu_ref[pk, rows, half:2 * half]
            x_re, x_im = n_re, n_im
            if packed_rows:
                xb_ref[pk, rows, 0:half] = x_re.astype(BF16)
                xb_ref[pk, rows, half:2 * half] = x_im.astype(BF16)
            else:
                bu_ref[pk, rows, 0:half] = x_re
                bu_ref[pk, rows, half:2 * half] = x_im
        st_ref[pk, :, 0:half] = x_re
        st_ref[pk, :, half:2 * half] = x_im
    sf_ref[...] = st_ref[...]
    for pk in range(2):
        xs = xb_ref[pk] if packed_rows else bu_ref[pk].astype(BF16)
        y = jnp.dot(xs, cc_ref[pk], preferred_element_type=F32)
        y = y + d_ref[pk] * us[pk].astype(F32)
        y_ref[:, pk * cin:(pk + 1) * cin] = jax.nn.gelu(y).astype(BF16)


def _s5_scan(u_tm, bb, cc, dvec, abar_re, abar_im, s0, *, nb, l):
    d_ssm = u_tm.shape[1]
    npk = bb.shape[0]
    cin = bb.shape[1]
    two_half = bb.shape[2]
    tc = min(SSM_TIME, l)
    rows = tc * nb
    assert npk % 2 == 0
    kern = functools.partial(_s5_scan_kernel, nb=nb, tc=tc, cin=cin)
    pair = lambda p, t: (p, 0, 0)
    return pl.pallas_call(
        kern,
        grid=(npk // 2, l // tc),
        in_specs=[pl.BlockSpec((rows, 2 * cin), lambda p, t: (t, p)),
                  pl.BlockSpec((2, cin, two_half), pair),
                  pl.BlockSpec((2, two_half, cin), pair),
                  pl.BlockSpec((2, 1, cin), pair),
                  pl.BlockSpec((2, 1, two_half // 2), pair),
                  pl.BlockSpec((2, 1, two_half // 2), pair),
                  pl.BlockSpec((2, nb, two_half), pair)],
        out_specs=[pl.BlockSpec((rows, 2 * cin), lambda p, t: (t, p)),
                   pl.BlockSpec((2, nb, two_half), pair)],
        out_shape=[jax.ShapeDtypeStruct((l * nb, d_ssm), BF16),
                   jax.ShapeDtypeStruct((npk, nb, two_half), F32)],
        scratch_shapes=[pltpu.VMEM((2, rows, two_half), F32), pltpu.VMEM((2, rows, two_half), BF16),
                        pltpu.VMEM((2, nb, two_half), F32)],
        compiler_params=_cparams(("arbitrary", "arbitrary"), VMEM_LIMIT),
        name="s5_scan",
    )(u_tm, bb, cc, dvec, abar_re, abar_im, s0)


def _s5_pack_weights(bb_re, bb_im, c_re, c_im, d, abar_re, abar_im):
    g, i, p = bb_re.shape
    npk = g // SSM_PACK
    eye = jnp.eye(SSM_PACK, dtype=F32)

    def blockdiag_in(m):
        m = m.reshape(npk, SSM_PACK, i, p)
        return jnp.einsum("ngip,gh->ngihp", m, eye).reshape(npk, SSM_PACK * i, SSM_PACK * p)

    def blockdiag_out(m):
        m = m.reshape(npk, SSM_PACK, i, p)
        return jnp.einsum("ngip,gh->ngphi", m, eye).reshape(npk, SSM_PACK * p, SSM_PACK * i)

    bb = jnp.concatenate([blockdiag_in(bb_re), blockdiag_in(bb_im)], axis=2).astype(BF16)
    cc = jnp.concatenate([blockdiag_out(c_re), -blockdiag_out(c_im)], axis=1).astype(BF16)
    dvec = d.reshape(npk, 1, SSM_PACK * i)
    are = abar_re.reshape(npk, 1, SSM_PACK * p)
    aim = abar_im.reshape(npk, 1, SSM_PACK * p)
    return bb, cc, dvec, are, aim


def _state_to_packs(s_re, s_im):
    b, g, p = s_re.shape
    npk = g // SSM_PACK
    re = jnp.transpose(s_re.reshape(b, npk, SSM_PACK * p), (1, 0, 2))
    im = jnp.transpose(s_im.reshape(b, npk, SSM_PACK * p), (1, 0, 2))
    return jnp.concatenate([re, im], axis=2)


def _packs_to_state(sf, g, p):
    npk, b, two_half = sf.shape
    half = two_half // 2
    re = jnp.transpose(sf[:, :, :half], (1, 0, 2)).reshape(b, g, p)
    im = jnp.transpose(sf[:, :, half:], (1, 0, 2)).reshape(b, g, p)
    return re, im


def _outproj_kernel(attn_ref, ssm_ref, x_ref, mod_ref, wglu_ref, bglu_ref, wout_ref, gpm_ref, gpf_ref,
                    wr_ref, br_ref, x1_ref, h2_ref, lg_ref):
    y = ssm_ref[...]
    z = jnp.dot(y, wglu_ref[...], preferred_element_type=F32) + bglu_ref[...]
    s = (y.astype(F32) * jax.nn.sigmoid(z)).astype(BF16)
    m = jnp.dot(attn_ref[0], wout_ref[0:D_ATTN, :], preferred_element_type=F32)
    m = m + jnp.dot(s, wout_ref[D_ATTN:, :], preferred_element_type=F32)
    x1 = x_ref[0] + mod_ref[0, 2:3, :] * _rms(m, gpm_ref[...])
    x1_ref[0] = x1
    h2 = (_rms(x1, gpf_ref[...]) * (1.0 + mod_ref[0, 4:5, :]) + mod_ref[0, 3:4, :]).astype(BF16)
    h2_ref[...] = h2
    lg_ref[...] = jnp.dot(h2, wr_ref[...], preferred_element_type=F32) + br_ref[...]


def _outproj(attn, ssm2d, x, mod, w_glu_bf, b_glu, w_out_bf, g_post_mix, g_pre_ffn, w_router_bf, b_router):
    b, l, d = x.shape
    tm = min(ROW_TILE, l)
    nt = l // tm
    d_ssm = w_glu_bf.shape[0]
    ne = w_router_bf.shape[1]
    c2 = lambda bi, j: (0, 0)
    return pl.pallas_call(
        _outproj_kernel,
        grid=(b, nt),
        in_specs=[pl.BlockSpec((1, tm, D_ATTN), lambda bi, j: (bi, j, 0)),
                  pl.BlockSpec((tm, d_ssm), lambda bi, j: (j, bi)),
                  pl.BlockSpec((1, tm, d), lambda bi, j: (bi, j, 0)),
                  pl.BlockSpec((1, 6, d), lambda bi, j: (bi, 0, 0)),
                  pl.BlockSpec(w_glu_bf.shape, c2),
                  pl.BlockSpec((1, d_ssm), c2),
                  pl.BlockSpec(w_out_bf.shape, c2),
                  pl.BlockSpec((1, d), c2),
                  pl.BlockSpec((1, d), c2),
                  pl.BlockSpec(w_router_bf.shape, c2),
                  pl.BlockSpec((1, ne), c2)],
        out_specs=[pl.BlockSpec((1, tm, d), lambda bi, j: (bi, j, 0)),
                   pl.BlockSpec((tm, d), lambda bi, j: (bi * nt + j, 0)),
                   pl.BlockSpec((tm, ne), lambda bi, j: (bi * nt + j, 0))],
        out_shape=[jax.ShapeDtypeStruct((b, l, d), F32),
                   jax.ShapeDtypeStruct((b * l, d), BF16),
                   jax.ShapeDtypeStruct((b * l, ne), F32)],
        compiler_params=_cparams(("arbitrary", "arbitrary"), VMEM_LIMIT),
        name="glu_outproj_norms_router",
    )(attn, ssm2d, x, mod, w_glu_bf, b_glu.reshape(1, d_ssm), w_out_bf, g_post_mix.reshape(1, d),
      g_pre_ffn.reshape(1, d), w_router_bf, b_router.reshape(1, ne))


def _router_kernel(lg_ref, tri_ref, info_ref, pc_ref):
    lg = lg_ref[...]
    ne, tb = lg.shape
    eidx = lax.broadcasted_iota(I32, (ne, tb), 0).astype(F32)
    work = lg
    sels, vals = [], []
    for _ in range(TOP_K):
        m = work.max(axis=0, keepdims=True)
        idx = jnp.where(work == m, eidx, float(ne)).min(axis=0, keepdims=True)
        sel = eidx == idx
        sels.append(sel)
        vals.append(m)
        work = jnp.where(sel, -jnp.inf, work)
    ex = [jnp.exp(v - vals[0]) for v in vals]
    den = ex[0] + ex[1] + ex[2] + ex[3]
    onehot = (sels[0] | sels[1] | sels[2] | sels[3]).astype(F32)
    cnt = onehot.sum(axis=1, keepdims=True).astype(I32)
    pc = ((cnt + (BF16_ROWS - 1)) >> 4) << 4
    pc_ref[0] = pc
    rank = jnp.dot(onehot.astype(BF16), tri_ref[...], preferred_element_type=F32)
    low = jnp.where(lax.broadcasted_iota(I32, (ne, LANES), 1) < lax.broadcasted_iota(I32, (ne, LANES), 0), 1.0, 0.0)
    pcb = jnp.concatenate([jnp.broadcast_to(pc.astype(F32), (ne, LANES)), jnp.zeros((LANES - ne, LANES), F32)], axis=0)
    off = jnp.dot(low.astype(BF16), pcb.astype(BF16), preferred_element_type=F32)[:, 0:1]
    posmat = rank + off
    for k in range(TOP_K):
        info_ref[k:k + 1, :] = jnp.where(sels[k], posmat, 0.0).sum(axis=0, keepdims=True)
        info_ref[TOP_K + k:TOP_K + k + 1, :] = ex[k] / den


def _router(logits_t, tb):
    ne, t = logits_t.shape
    nb = t // tb
    tri = jnp.triu(jnp.ones((tb, tb), BF16), k=1)
    return pl.pallas_call(
        _router_kernel,
        grid=(nb,),
        in_specs=[pl.BlockSpec((ne, tb), lambda i: (0, i)),
                  pl.BlockSpec((tb, tb), lambda i: (0, 0))],
        out_specs=[pl.BlockSpec((2 * TOP_K, tb), lambda i: (0, i)),
                   pl.BlockSpec((1, ne, 1), lambda i: (i, 0, 0))],
        out_shape=[jax.ShapeDtypeStruct((2 * TOP_K, t), F32),
                   jax.ShapeDtypeStruct((nb, ne, 1), I32)],
        compiler_params=_cparams(("arbitrary",)),
        name="router_top4",
    )(logits_t, tri)


def _local_chunks(tb, ne):
    rows = TOP_K * tb + ne * (BF16_ROWS - 1)
    return -(-rows // SORT_ROWS) * (SORT_ROWS // BF16_ROWS)


def _sort_plan(pc, tb, n_tokens):
    nb, ne = pc.shape
    nch = _local_chunks(tb, ne)
    n_tiles = (TOP_K * n_tokens + nb * ne * (BF16_ROWS - 1)) // FFN_TILE + ne
    loc_end = jnp.cumsum(pc, axis=1)
    loc_off = loc_end - pc
    tot = pc.sum(axis=0)
    tot_pad = ((tot + FFN_TILE - 1) // FFN_TILE) * FFN_TILE
    base_end = jnp.cumsum(tot_pad)
    base = base_end - tot_pad
    glob = base[None, :] + jnp.cumsum(pc, axis=0) - pc
    n_used = (base_end[-1] // FFN_TILE).astype(I32)
    tile_row = jnp.arange(n_tiles, dtype=I32) * FFN_TILE
    tile_expert = jnp.minimum((base_end[None, :] <= tile_row[:, None]).sum(axis=1), ne - 1).astype(I32)
    r = jnp.arange(nch, dtype=I32) * BF16_ROWS
    e_of = jnp.minimum((loc_end[:, None, :] <= r[None, :, None]).sum(axis=2), ne - 1)
    shift = glob - loc_off
    owner = e_of[:, :, None] == jnp.arange(ne, dtype=I32)[None, None, :]
    dst = jnp.where(owner, shift[:, None, :], 0).sum(axis=2) + r[None, :]
    chunk_dst = (dst // BF16_ROWS).astype(I32)
    n_valid = (loc_end[:, -1] // BF16_ROWS).astype(I32)
    tail_n = ((tot_pad - tot) // BF16_ROWS).astype(I32)
    tail_dst = ((base + tot) // BF16_ROWS).astype(I32)
    return dict(nch=nch, n_tiles=n_tiles, chunk_dst=chunk_dst.reshape(-1), n_valid=n_valid, n_used=n_used,
                tile_expert=tile_expert, tail_n=tail_n, tail_dst=tail_dst)


def _chunk_copy(buf, hbm, slot, c, dst_chunk, sem, to_hbm):
    v = buf.at[slot, pl.ds(pl.multiple_of(c * BF16_ROWS, BF16_ROWS), BF16_ROWS)]
    h = hbm.at[pl.ds(pl.multiple_of(dst_chunk * BF16_ROWS, BF16_ROWS), BF16_ROWS)]
    return pltpu.make_async_copy(v, h, sem) if to_hbm else pltpu.make_async_copy(h, v, sem)


def _local_row_ids(rt, tb):
    return lax.broadcasted_iota(I32, (rt, tb), 0).astype(F32).astype(BF16)


def _local_pos(pos_row, r0, rt):
    return jnp.clip(pos_row - float(r0), -1.0, float(rt)).astype(BF16)


def _dispatch_kernel(cd_ref, nv_ref, tn_ref, td_ref, nu_ref, h_ref, info_ref, hx_ref, infox_ref, xs_ref,
                     buf_ref, zero_ref, ztile_ref, sem, zsem, *, nch, rt, n_tiles):
    b = pl.program_id(0)
    nb = pl.num_programs(0)
    slot = b % 2

    def sort_block(hr, ir):
        tb = hr.shape[0]
        h = hr[...]
        rid = _local_row_ids(rt, tb)
        one = jnp.ones((rt, tb), BF16)
        for r0 in range(0, _local_chunks(tb, N_EXPERTS) * BF16_ROWS, rt):
            p = jnp.zeros((rt, tb), BF16)
            for k in range(TOP_K):
                p = jnp.where(rid == _local_pos(ir[k:k + 1, :], r0, rt), one, p)
            buf_ref[slot, r0:r0 + rt, :] = jnp.dot(p, h, preferred_element_type=F32).astype(BF16)

    pl.when(b < nb - 1)(lambda: sort_block(h_ref, info_ref))
    pl.when(b == nb - 1)(lambda: sort_block(hx_ref, infox_ref))

    def issue(c, _):
        _chunk_copy(buf_ref, xs_ref, slot, c, cd_ref[b * nch + c], sem.at[slot], True).start()
        return 0

    lax.fori_loop(0, nv_ref[b], issue, 0)

    def drain(step, sl):
        def wait(c, _):
            _chunk_copy(buf_ref, xs_ref, sl, c, cd_ref[step * nch + c], sem.at[sl], True).wait()
            return 0

        lax.fori_loop(0, nv_ref[step], wait, 0)

    @pl.when(b > 0)
    def _():
        drain(b - 1, 1 - slot)

    @pl.when(b == nb - 1)
    def _():
        drain(b, slot)
        zero_ref[...] = jnp.zeros_like(zero_ref)
        for e in range(N_EXPERTS):
            def zcopy(c, e=e):
                dst = pl.multiple_of((td_ref[e] + c) * BF16_ROWS, BF16_ROWS)
                return pltpu.make_async_copy(zero_ref, xs_ref.at[pl.ds(dst, BF16_ROWS)], zsem)

            lax.fori_loop(0, tn_ref[e], lambda c, _: (zcopy(c).start(), 0)[1], 0)
            lax.fori_loop(0, tn_ref[e], lambda c, _: (zcopy(c).wait(), 0)[1], 0)

        ztile_ref[...] = jnp.zeros_like(ztile_ref)

        def ztile(i):
            dst = pl.multiple_of(i * FFN_TILE, FFN_TILE)
            return pltpu.make_async_copy(ztile_ref, xs_ref.at[pl.ds(dst, FFN_TILE)], zsem)

        lax.fori_loop(nu_ref[0], n_tiles, lambda i, _: (ztile(i).start(), 0)[1], 0)
        lax.fori_loop(nu_ref[0], n_tiles, lambda i, _: (ztile(i).wait(), 0)[1], 0)


def _dispatch(h2, info, h2x, infox, plan, tb):
    t, d = h2.shape
    tx = h2x.shape[0]
    nb = t // tb
    nch = plan["nch"]
    rows = plan["n_tiles"] * FFN_TILE
    assert (nch * BF16_ROWS) % SORT_ROWS == 0 and tx <= tb
    kern = functools.partial(_dispatch_kernel, nch=nch, rt=SORT_ROWS, n_tiles=plan["n_tiles"])
    grid_spec = pltpu.PrefetchScalarGridSpec(
        num_scalar_prefetch=5,
        grid=(nb + 1,),
        in_specs=[pl.BlockSpec((tb, d), lambda i, *_: (jnp.minimum(i, nb - 1), 0)),
                  pl.BlockSpec((2 * TOP_K, tb), lambda i, *_: (0, jnp.minimum(i, nb - 1))),
                  pl.BlockSpec((tx, d), lambda i, *_: (0, 0)),
                  pl.BlockSpec((2 * TOP_K, tx), lambda i, *_: (0, 0))],
        out_specs=pl.BlockSpec(memory_space=pl.ANY),
        scratch_shapes=[pltpu.VMEM((2, nch * BF16_ROWS, d), BF16),
                        pltpu.VMEM((BF16_ROWS, d), BF16),
                        pltpu.VMEM((FFN_TILE, d), BF16),
                        pltpu.SemaphoreType.DMA((2,)),
                        pltpu.SemaphoreType.DMA],
    )
    return pl.pallas_call(
        kern,
        grid_spec=grid_spec,
        out_shape=jax.ShapeDtypeStruct((rows, d), BF16),
        compiler_params=_cparams(("arbitrary",), VMEM_LIMIT),
        name="moe_dispatch",
    )(plan["chunk_dst"], plan["n_valid"], plan["tail_n"], plan["tail_dst"], plan["n_used"].reshape(1), h2, info,
      h2x, infox)


def _ffn_kernel(te_ref, nu_ref, x_ref, wgu_ref, bgu_ref, wd_ref, bd_ref, perm_ref, y_ref, wgu_s, wd_s):
    i = pl.program_id(0)
    f = wd_s.shape[0]
    prev = te_ref[jnp.maximum(i - 1, 0)]
    new_expert = (i == 0) | (te_ref[i] != prev)

    @pl.when((i < nu_ref[0]) & new_expert)
    def _():
        for c in range(0, 2 * f, 2 * LANES):
            blk = wgu_ref[0, :, c:c + 2 * LANES].astype(BF16)
            wgu_s[:, c:c + 2 * LANES] = jnp.dot(blk, perm_ref[...], preferred_element_type=F32).astype(BF16)
        wd_s[...] = wd_ref[0].astype(BF16)

    @pl.when(i < nu_ref[0])
    def _():
        x = x_ref[...]
        acts = []
        for c in range(0, 2 * f, 2 * LANES):
            gu = jnp.dot(x, wgu_s[:, c:c + 2 * LANES], preferred_element_type=F32) + bgu_ref[0, :, c:c + 2 * LANES]
            glu = jnp.minimum(gu[:, :LANES], SWIGLU_LIMIT)
            lin = jnp.clip(gu[:, LANES:], -SWIGLU_LIMIT, SWIGLU_LIMIT)
            acts.append((glu * jax.nn.sigmoid(SWIGLU_ALPHA * glu) * (lin + 1.0)).astype(BF16))
        act = jnp.concatenate(acts, axis=1)
        y_ref[...] = (jnp.dot(act, wd_s[...], preferred_element_type=F32) + bd_ref[0]).astype(BF16)

    @pl.when(i >= nu_ref[0])
    def _():
        y_ref[...] = jnp.zeros_like(y_ref)


def _ffn(xs, plan, w_gate_up, b_gu_perm, w_down, b_down):
    rows, d = xs.shape
    ne, _, f2 = w_gate_up.shape
    f = f2 // 2
    n_tiles = plan["n_tiles"]
    src = jnp.concatenate([jnp.arange(LANES) * 2, jnp.arange(LANES) * 2 + 1])
    perm = (jnp.arange(2 * LANES)[:, None] == src[None, :]).astype(BF16)

    def xi(i, te, nu):
        return (jnp.minimum(i, nu[0] - 1), 0)

    def wi(i, te, nu):
        return (te[jnp.minimum(i, nu[0] - 1)], 0, 0)

    grid_spec = pltpu.PrefetchScalarGridSpec(
        num_scalar_prefetch=2,
        grid=(n_tiles,),
        in_specs=[pl.BlockSpec((FFN_TILE, d), xi),
                  pl.BlockSpec((1, d, f2), wi),
                  pl.BlockSpec((1, 1, f2), wi),
                  pl.BlockSpec((1, f, d), wi),
                  pl.BlockSpec((1, 1, d), wi),
                  pl.BlockSpec((2 * LANES, 2 * LANES), lambda i, te, nu: (0, 0))],
        out_specs=pl.BlockSpec((FFN_TILE, d), lambda i, te, nu: (i, 0)),
        scratch_shapes=[pltpu.VMEM((d, f2), BF16), pltpu.VMEM((f, d), BF16)],
    )
    return pl.pallas_call(
        _ffn_kernel,
        grid_spec=grid_spec,
        out_shape=jax.ShapeDtypeStruct((rows, d), BF16),
        compiler_params=_cparams(("arbitrary",), VMEM_LIMIT),
        name="moe_expert_ffn",
    )(plan["tile_expert"], plan["n_used"].reshape(1), xs, w_gate_up, b_gu_perm.reshape(ne, 1, f2), w_down,
      b_down.reshape(ne, 1, d), perm)


def _combine_kernel(cd_ref, nv_ref, ys_ref, info_ref, x1_ref, mod_ref, infox_ref, x1x_ref, modx_ref, g_ref,
                    o_ref, ox_ref, buf_ref, w_ref, sem, *, nch, rt):
    b = pl.program_id(0)
    nb = pl.num_programs(0)

    @pl.when(b == 0)
    def _():
        buf_ref[...] = jnp.zeros_like(buf_ref)

    def fetch(c, _):
        _chunk_copy(buf_ref, ys_ref, 0, c, cd_ref[b * nch + c], sem, False).start()
        return 0

    lax.fori_loop(0, nv_ref[b], fetch, 0)

    def unsort_block(ir, x1r, modr, outr):
        tb = x1r.shape[0]
        nbat = modr.shape[0]
        rb = tb // nbat
        nrows = _local_chunks(tb, N_EXPERTS) * BF16_ROWS
        rid = _local_row_ids(rt, tb)
        for r0 in range(0, nrows, rt):
            w = jnp.zeros((rt, tb), BF16)
            for k in range(TOP_K):
                gate = jnp.broadcast_to(ir[TOP_K + k:TOP_K + k + 1, :].astype(BF16), (rt, tb))
                w = jnp.where(rid == _local_pos(ir[k:k + 1, :], r0, rt), gate, w)
            w_ref[r0:r0 + rt, 0:tb] = w

        def wait(c, _):
            _chunk_copy(buf_ref, ys_ref, 0, c, cd_ref[b * nch + c], sem, False).wait()
            return 0

        lax.fori_loop(0, nv_ref[b], wait, 0)

        moe = lax.dot_general(w_ref[0:nrows, 0:tb], buf_ref[0, 0:nrows, :], (((0,), (0,)), ((), ())),
                              preferred_element_type=F32)
        out = _rms(moe, g_ref[...])
        for s in range(nbat):
            rs = slice(s * rb, (s + 1) * rb)
            outr[rs, :] = x1r[rs, :] + modr[s, 5:6, :] * out[rs, :]

    pl.when(b < nb - 1)(lambda: unsort_block(info_ref, x1_ref, mod_ref, o_ref))
    pl.when(b == nb - 1)(lambda: unsort_block(infox_ref, x1x_ref, modx_ref, ox_ref))


def _combine(ys, info, x1, mod, infox, x1x, modx, g_post_ffn, plan, tb, l):
    t, d = x1.shape
    tx = x1x.shape[0]
    nb = t // tb
    nch = plan["nch"]
    assert l % tb == 0
    clamp = lambda i: jnp.minimum(i, nb - 1)
    kern = functools.partial(_combine_kernel, nch=nch, rt=SORT_ROWS)
    grid_spec = pltpu.PrefetchScalarGridSpec(
        num_scalar_prefetch=2,
        grid=(nb + 1,),
        in_specs=[pl.BlockSpec(memory_space=pl.ANY),
                  pl.BlockSpec((2 * TOP_K, tb), lambda i, *_: (0, clamp(i))),
                  pl.BlockSpec((tb, d), lambda i, *_: (clamp(i), 0)),
                  pl.BlockSpec((1, 6, d), lambda i, *_: (clamp(i) // (l // tb), 0, 0)),
                  pl.BlockSpec((2 * TOP_K, tx), lambda i, *_: (0, 0)),
                  pl.BlockSpec((tx, d), lambda i, *_: (0, 0)),
                  pl.BlockSpec(modx.shape, lambda i, *_: (0, 0, 0)),
                  pl.BlockSpec((1, d), lambda i, *_: (0, 0))],
        out_specs=[pl.BlockSpec((tb, d), lambda i, *_: (clamp(i), 0)),
                   pl.BlockSpec((tx, d), lambda i, *_: (0, 0))],
        scratch_shapes=[pltpu.VMEM((1, nch * BF16_ROWS, d), BF16), pltpu.VMEM((nch * BF16_ROWS, tb), BF16),
                        pltpu.SemaphoreType.DMA],
    )
    return pl.pallas_call(
        kern,
        grid_spec=grid_spec,
        out_shape=[jax.ShapeDtypeStruct((t, d), F32), jax.ShapeDtypeStruct((tx, d), F32)],
        compiler_params=_cparams(("arbitrary",), VMEM_LIMIT),
        name="moe_combine",
    )(plan["chunk_dst"], plan["n_valid"], ys, info, x1, mod, infox, x1x, modx, g_post_ffn.reshape(1, d))


def _moe(main, second, l, g_post_ffn, w_gate_up, b_gu_perm, w_down, b_down):
    h2, logits, x1, mod = main
    h2x, logitsx, x1x, modx = second
    t = h2.shape[0]
    tx = h2x.shape[0]
    tb = ROW_TILE
    info, pc = _router(logits[:, :N_EXPERTS].T, tb)
    infox, pcx = _router(logitsx[:, :N_EXPERTS].T, tx)
    pc_all = jnp.concatenate([pc.reshape(t // tb, N_EXPERTS), pcx.reshape(1, N_EXPERTS)], axis=0)
    plan = _sort_plan(pc_all, tb, t + tx)
    xs = _dispatch(h2, info, h2x, infox, plan, tb)
    ys = _ffn(xs, plan, w_gate_up, b_gu_perm, w_down, b_down)
    return _combine(ys, info, x1, mod, infox, x1x, modx, g_post_ffn, plan, tb, l)


def _stream(x, mod, wts, attn_fn, s0_packs):
    b, l, d = x.shape
    q, k_att, v_att, u2d, kf, vf = _inproj(x, mod, wts["g_pre_mix"], wts["w_in"], pad_front=attn_fn is None)
    if attn_fn is None:
        attn = _attn_prompt(q, k_att, v_att, wts["table_prompt"])
    else:
        attn = attn_fn(q, k_att, v_att)
    d_ssm = u2d.shape[1] // b
    u_tm = u2d.reshape(l * b, d_ssm)
    y_tm, sf = _s5_scan(u_tm, wts["bb"], wts["cc"], wts["dvec"], wts["are"], wts["aim"], s0_packs, nb=b, l=l)
    ssm2d = y_tm.reshape(l, b * d_ssm)
    x1, h2, logits = _outproj(attn, ssm2d, x, mod, wts["w_glu"], wts["b_glu"], wts["w_out"], wts["g_post_mix"],
                              wts["g_pre_ffn"], wts["w_router"], wts["b_router"])
    return (h2, logits, x1.reshape(b * l, d), mod), kf, vf, sf


def kernel(x_prompt, x_sample, cache_k, cache_v, state_ssm_re, state_ssm_im, c_prompt, c_sample, w_ada, b_ada, g_pre_mix, g_post_mix, g_pre_ffn, g_post_ffn, w_in, rel_bias, ssm_a_re, ssm_a_im, ssm_log_dt, ssm_b_re, ssm_b_im, ssm_c_re, ssm_c_im, ssm_d, w_glu, b_glu, w_out, w_router, b_router, w_gate_up, b_gate_up, w_down, b_down):
    depth = w_ada.shape[0]
    assert depth == 1, "single-layer trunk"
    bp, lp, d = x_prompt.shape
    bs, ls, _ = x_sample.shape
    n_cache = cache_k.shape[2]
    past_len = PAST_LEN
    g, p = ssm_a_re.shape[1:]
    lyr = 0

    mod = _ada(jnp.concatenate([c_prompt, c_sample], axis=0), w_ada[lyr], b_ada[lyr])
    mod = mod.reshape(bp + bs, 6, d)

    abre, abim, bbre, bbim = _s5_prep(ssm_a_re[lyr], ssm_a_im[lyr], ssm_log_dt[lyr], ssm_b_re[lyr], ssm_b_im[lyr])
    bb, cc, dvec, are, aim = _s5_pack_weights(bbre, bbim, ssm_c_re[lyr], ssm_c_im[lyr], ssm_d[lyr], abre, abim)

    f2 = w_gate_up.shape[-1]
    b_gu_perm = b_gate_up[lyr].reshape(N_EXPERTS, f2 // (2 * LANES), LANES, 2)
    b_gu_perm = jnp.transpose(b_gu_perm, (0, 1, 3, 2)).reshape(N_EXPERTS, f2)

    rb = rel_bias[lyr]
    wts = dict(
        g_pre_mix=g_pre_mix[lyr], g_post_mix=g_post_mix[lyr], g_pre_ffn=g_pre_ffn[lyr], g_post_ffn=g_post_ffn[lyr],
        w_in=w_in[lyr].astype(BF16), w_glu=w_glu[lyr].astype(BF16), b_glu=b_glu[lyr], w_out=w_out[lyr].astype(BF16),
        w_router=jnp.pad(w_router[lyr], ((0, 0), (0, LANES - N_EXPERTS))).astype(BF16),
        b_router=jnp.pad(b_router[lyr], (0, LANES - N_EXPERTS)),
        w_gate_up=w_gate_up[lyr], b_gu_perm=b_gu_perm, w_down=w_down[lyr], b_down=b_down[lyr],
        bb=bb, cc=cc, dvec=dvec, are=are, aim=aim,
        table_prompt=_bias_table(rb, q0=WIN_ROWS, k0=0, nq=CHUNK, nk=Q_TILE + WIN_ROWS, strips=Q_TILE // CHUNK),
    )

    zero_state = jnp.zeros((g // SSM_PACK, bp, 2 * SSM_PACK * p), F32)
    ffn_in_p, kf_p, vf_p, sf_p = _stream(x_prompt, mod[:bp], wts, None, zero_state)

    tabc = _bias_table(rb, q0=past_len, k0=past_len - n_cache, nq=ls, nk=n_cache)
    tabn = _bias_table(rb, q0=past_len, k0=past_len, nq=ls, nk=ls)
    ck = cache_k[lyr].reshape(bs, n_cache, D_ATTN)
    cv = cache_v[lyr].reshape(bs, n_cache, D_ATTN)
    attn_s = lambda q, k, v: _attn_sample(q, k, v, ck, cv, tabc, tabn)
    s0 = _state_to_packs(state_ssm_re[lyr], state_ssm_im[lyr])
    ffn_in_s, kf_s, vf_s, sf_s = _stream(x_sample, mod[bp:], wts, attn_s, s0)

    y_p, y_s = _moe(ffn_in_p, ffn_in_s, lp, wts["g_post_ffn"], wts["w_gate_up"], wts["b_gu_perm"], wts["w_down"],
                    wts["b_down"])
    y_p = y_p.reshape(bp, lp, d)
    y_s = y_s.reshape(bs, ls, d)

    def heads(a):
        return a.reshape(1, a.shape[0], a.shape[1], N_HEADS, HEAD_DIM)

    sre_p, sim_p = _packs_to_state(sf_p, g, p)
    sre_s, sim_s = _packs_to_state(sf_s, g, p)
    return (y_p, y_s, heads(kf_p), heads(vf_p), sre_p[None], sim_p[None],
            heads(kf_s), heads(vf_s), sre_s[None], sim_s[None])
```

```python
import functools
import math

import jax
import jax.numpy as jnp
from jax import lax
from jax.experimental import pallas as pl
from jax.experimental.pallas import tpu as pltpu

F32 = jnp.float32
BF16 = jnp.bfloat16
I32 = jnp.int32
HIGHEST = lax.Precision.HIGHEST

CHUNK = 64
CHUNK_LOG2 = 6
N_PAST_CHUNKS = 8
PAST_LEN = 2048
WIN_ROWS = N_PAST_CHUNKS * CHUNK
MAX_REL = 64
N_REL = 2 * MAX_REL + 1
N_HEADS = 8
HEAD_DIM = 64
D_ATTN = N_HEADS * HEAD_DIM
SSM_GROUP = 16
SSM_STATE = 64
N_EXPERTS = 32
TOP_K = 4
SWIGLU_LIMIT = 7.0
SWIGLU_ALPHA = 1.702
RMS_EPS = 1e-6
NEG_INF = -1e30
LOG2E = math.log2(math.e)

LANES = 128
BF16_ROWS = 16
VMEM_LIMIT = 56 * 1024 * 1024

ROW_TILE = 512
Q_TILE = 256
SSM_PACK = 8
SSM_TIME = 64
FFN_TILE = 512
SORT_ROWS = 256


def _cparams(sem, vmem=None):
    return pltpu.CompilerParams(dimension_semantics=sem, vmem_limit_bytes=vmem)


def _rms(x, g):
    return x * lax.rsqrt(jnp.mean(x * x, axis=-1, keepdims=True) + RMS_EPS) * g


def _ada_kernel(c_ref, w_ref, b_ref, o_ref):
    c = c_ref[...]
    s = c * jax.nn.sigmoid(c)
    o_ref[...] = jnp.dot(s, w_ref[...], precision=HIGHEST, preferred_element_type=F32) + b_ref[...]


def _ada(c, w_ada, b_ada):
    n, d = c.shape
    nout = w_ada.shape[1]
    return pl.pallas_call(
        _ada_kernel,
        grid=(nout // d,),
        in_specs=[pl.BlockSpec((n, d), lambda j: (0, 0)),
                  pl.BlockSpec((d, d), lambda j: (0, j)),
                  pl.BlockSpec((1, d), lambda j: (0, j))],
        out_specs=pl.BlockSpec((n, d), lambda j: (0, j)),
        out_shape=jax.ShapeDtypeStruct((n, nout), F32),
        compiler_params=_cparams(("arbitrary",), VMEM_LIMIT),
        name="ada_mod",
    )(c, w_ada, b_ada.reshape(1, nout))


def _inproj_kernel(x_ref, mod_ref, g_ref, w_ref, q_ref, k_ref, v_ref, u_ref, kf_ref, vf_ref, *, first_keep, pad):
    j = pl.program_id(1)

    def compute():
        x = x_ref[0]
        h = _rms(x, g_ref[...]) * (1.0 + mod_ref[0, 1:2, :]) + mod_ref[0, 0:1, :]
        proj = jnp.dot(h.astype(BF16), w_ref[...], preferred_element_type=F32)
        q_ref[0] = (proj[:, :D_ATTN] * (HEAD_DIM ** -0.5 * LOG2E)).astype(BF16)
        k_ref[0] = proj[:, D_ATTN:2 * D_ATTN].astype(BF16)
        v_ref[0] = proj[:, 2 * D_ATTN:3 * D_ATTN].astype(BF16)
        u_ref[...] = proj[:, 3 * D_ATTN:].astype(BF16)

        @pl.when(j >= first_keep + pad)
        def _():
            kf_ref[0] = proj[:, D_ATTN:2 * D_ATTN]
            vf_ref[0] = proj[:, 2 * D_ATTN:3 * D_ATTN]

    if pad:
        @pl.when(j == 0)
        def _():
            k_ref[...] = jnp.zeros_like(k_ref)
            v_ref[...] = jnp.zeros_like(v_ref)

        pl.when(j > 0)(compute)
    else:
        compute()


def _inproj(x, mod, g, w_in_bf, *, pad_front):
    b, l, d = x.shape
    tm = min(ROW_TILE, l)
    nt = l // tm
    keep = min(WIN_ROWS, l)
    first_keep = (l - keep) // tm
    pad = 1 if pad_front else 0
    d_ssm = w_in_bf.shape[1] - 3 * D_ATTN
    assert l % tm == 0 and keep % tm == 0 and (not pad_front or tm == WIN_ROWS)

    def xi(bi, j):
        return (bi, jnp.maximum(j - pad, 0), 0)

    def kfi(bi, j):
        return (bi, jnp.maximum(j - pad - first_keep, 0), 0)

    kern = functools.partial(_inproj_kernel, first_keep=first_keep, pad=pad)
    return pl.pallas_call(
        kern,
        grid=(b, nt + pad),
        in_specs=[pl.BlockSpec((1, tm, d), xi),
                  pl.BlockSpec((1, 6, d), lambda bi, j: (bi, 0, 0)),
                  pl.BlockSpec((1, d), lambda bi, j: (0, 0)),
                  pl.BlockSpec(w_in_bf.shape, lambda bi, j: (0, 0))],
        out_specs=[pl.BlockSpec((1, tm, D_ATTN), xi),
                   pl.BlockSpec((1, tm, D_ATTN), lambda bi, j: (bi, j, 0)),
                   pl.BlockSpec((1, tm, D_ATTN), lambda bi, j: (bi, j, 0)),
                   pl.BlockSpec((tm, d_ssm), lambda bi, j: (jnp.maximum(j - pad, 0), bi)),
                   pl.BlockSpec((1, tm, D_ATTN), kfi),
                   pl.BlockSpec((1, tm, D_ATTN), kfi)],
        out_shape=[jax.ShapeDtypeStruct((b, l, D_ATTN), BF16),
                   jax.ShapeDtypeStruct((b, l + pad * tm, D_ATTN), BF16),
                   jax.ShapeDtypeStruct((b, l + pad * tm, D_ATTN), BF16),
                   jax.ShapeDtypeStruct((l, b * d_ssm), BF16),
                   jax.ShapeDtypeStruct((b, keep, D_ATTN), F32),
                   jax.ShapeDtypeStruct((b, keep, D_ATTN), F32)],
        compiler_params=_cparams(("arbitrary", "arbitrary"), VMEM_LIMIT),
        name="prenorm_inproj",
    )(x, mod, g.reshape(1, d), w_in_bf)


def _bias_kernel(rb_ref, o_ref, *, q0, k0, nq, nk, strips):
    h = pl.program_id(0)
    qpos = q0 + lax.broadcasted_iota(I32, (nq, nk), 0)
    kpos = k0 + lax.broadcasted_iota(I32, (nq, nk), 1)
    rel = jnp.clip(kpos - qpos, -MAX_REL, MAX_REL) + MAX_REL
    qc = qpos >> CHUNK_LOG2
    kc = kpos >> CHUNK_LOG2
    allowed = (kc <= qc) & (kc >= qc - N_PAST_CHUNKS)

    def body(r, acc):
        return jnp.where(rel == r, rb_ref[r, h], acc)

    val = lax.fori_loop(0, N_REL, body, jnp.zeros((nq, nk), F32))
    strip = jnp.where(allowed, val * LOG2E, NEG_INF)
    o_ref[0, 0:nq, :] = strip
    for c in range(1, strips):
        o_ref[0, c * nq:(c + 1) * nq, :] = pltpu.roll(strip, c * CHUNK, axis=1)


def _bias_table(rel_bias, *, q0, k0, nq, nk, strips=1):
    kern = functools.partial(_bias_kernel, q0=q0, k0=k0, nq=nq, nk=nk, strips=strips)
    return pl.pallas_call(
        kern,
        grid=(N_HEADS,),
        in_specs=[pl.BlockSpec(memory_space=pltpu.SMEM)],
        out_specs=pl.BlockSpec((1, strips * nq, nk), lambda h: (h, 0, 0)),
        out_shape=jax.ShapeDtypeStruct((N_HEADS, strips * nq, nk), F32),
        compiler_params=_cparams(("arbitrary",)),
        name="bias_table",
    )(rel_bias)


def _softmax_pv(scores, values):
    m = scores[0].max(axis=-1, keepdims=True)
    for s in scores[1:]:
        m = jnp.maximum(m, s.max(axis=-1, keepdims=True))
    acc = None
    den = None
    for s, v in zip(scores, values):
        p = jnp.exp2(s - m)
        d = p.sum(axis=-1, keepdims=True)
        o = jnp.dot(p.astype(BF16), v, preferred_element_type=F32)
        acc = o if acc is None else acc + o
        den = d if den is None else den + d
    return acc / den


def _attn_prompt_kernel(q_ref, k_ref, v_ref, tab_ref, o_ref, *, tq, band):
    qi = pl.program_id(1)
    t0 = pl.multiple_of(qi * tq, tq)
    lane = lax.broadcasted_iota(I32, (1, LANES), 1)

    def body(mask_front):
        if mask_front:
            col = lax.broadcasted_iota(I32, (1, band), 1)
            colmask = jnp.where(col + t0 >= WIN_ROWS, 0.0, NEG_INF).astype(F32)
        for hp in range(N_HEADS // 2):
            ls = slice(hp * LANES, (hp + 1) * LANES)
            q2 = q_ref[0, :, ls]
            k2 = k_ref[0, pl.ds(t0, band), ls]
            v2 = v_ref[0, pl.ds(t0, band), ls]
            outs = []
            for a in range(2):
                qa = jnp.where((lane < HEAD_DIM) == (a == 0), q2, jnp.zeros_like(q2))
                s = lax.dot_general(qa, k2, (((1,), (1,)), ((), ())), preferred_element_type=F32)
                s = s + tab_ref[2 * hp + a]
                if mask_front:
                    s = s + colmask
                outs.append(_softmax_pv([s], [v2]))
            o_ref[0, :, ls] = jnp.where(lane < HEAD_DIM, outs[0], outs[1]).astype(BF16)

    pl.when(t0 < WIN_ROWS)(lambda: body(True))
    pl.when(t0 >= WIN_ROWS)(lambda: body(False))


def _attn_prompt(q, k_pad, v_pad, table):
    b, l, _ = q.shape
    tq = Q_TILE
    band = tq + WIN_ROWS
    kern = functools.partial(_attn_prompt_kernel, tq=tq, band=band)
    return pl.pallas_call(
        kern,
        grid=(b, l // tq),
        in_specs=[pl.BlockSpec((1, tq, D_ATTN), lambda bi, i: (bi, i, 0)),
                  pl.BlockSpec((1, l + WIN_ROWS, D_ATTN), lambda bi, i: (bi, 0, 0)),
                  pl.BlockSpec((1, l + WIN_ROWS, D_ATTN), lambda bi, i: (bi, 0, 0)),
                  pl.BlockSpec(table.shape, lambda bi, i: (0, 0, 0))],
        out_specs=pl.BlockSpec((1, tq, D_ATTN), lambda bi, i: (bi, i, 0)),
        out_shape=jax.ShapeDtypeStruct((b, l, D_ATTN), BF16),
        compiler_params=_cparams(("arbitrary", "arbitrary"), VMEM_LIMIT),
        name="attn_prompt",
    )(q, k_pad, v_pad, table)


def _attn_sample_kernel(q_ref, k_ref, v_ref, ck_ref, cv_ref, tabc_ref, tabn_ref, o_ref):
    lane = lax.broadcasted_iota(I32, (1, LANES), 1)
    for hp in range(N_HEADS // 2):
        ls = slice(hp * LANES, (hp + 1) * LANES)
        q2 = q_ref[0, :, ls]
        kn = k_ref[0, :, ls]
        vn = v_ref[0, :, ls]
        kc = ck_ref[0, :, ls].astype(BF16)
        vc = cv_ref[0, :, ls].astype(BF16)
        outs = []
        for a in range(2):
            h = 2 * hp + a
            qa = jnp.where((lane < HEAD_DIM) == (a == 0), q2, jnp.zeros_like(q2))
            sc = lax.dot_general(qa, kc, (((1,), (1,)), ((), ())), preferred_element_type=F32) + tabc_ref[h]
            sn = lax.dot_general(qa, kn, (((1,), (1,)), ((), ())), preferred_element_type=F32) + tabn_ref[h]
            outs.append(_softmax_pv([sc, sn], [vc, vn]))
        o_ref[0, :, ls] = jnp.where(lane < HEAD_DIM, outs[0], outs[1]).astype(BF16)


def _attn_sample(q, k, v, cache_k, cache_v, tabc, tabn):
    b, s, _ = q.shape
    nc = cache_k.shape[1]
    return pl.pallas_call(
        _attn_sample_kernel,
        grid=(b,),
        in_specs=[pl.BlockSpec((1, s, D_ATTN), lambda bi: (bi, 0, 0)),
                  pl.BlockSpec((1, s, D_ATTN), lambda bi: (bi, 0, 0)),
                  pl.BlockSpec((1, s, D_ATTN), lambda bi: (bi, 0, 0)),
                  pl.BlockSpec((1, nc, D_ATTN), lambda bi: (bi, 0, 0)),
                  pl.BlockSpec((1, nc, D_ATTN), lambda bi: (bi, 0, 0)),
                  pl.BlockSpec(tabc.shape, lambda bi: (0, 0, 0)),
                  pl.BlockSpec(tabn.shape, lambda bi: (0, 0, 0))],
        out_specs=pl.BlockSpec((1, s, D_ATTN), lambda bi: (bi, 0, 0)),
        out_shape=jax.ShapeDtypeStruct((b, s, D_ATTN), BF16),
        compiler_params=_cparams(("arbitrary",), VMEM_LIMIT),
        name="attn_sample",
    )(q, k, v, cache_k, cache_v, tabc, tabn)


def _s5_prep_kernel(are_ref, aim_ref, ldt_ref, bre_ref, bim_ref, abre_ref, abim_ref, bbre_ref, bbim_ref):
    a_re = are_ref[...]
    a_im = aim_ref[...]
    dt = jnp.exp(ldt_ref[...])
    mag = jnp.exp(dt * a_re)
    abar_re = mag * jnp.cos(dt * a_im)
    abar_im = mag * jnp.sin(dt * a_im)
    den = a_re * a_re + a_im * a_im
    num_re = abar_re - 1.0
    coef_re = (num_re * a_re + abar_im * a_im) / den
    coef_im = (abar_im * a_re - num_re * a_im) / den
    abre_ref[...] = abar_re
    abim_ref[...] = abar_im
    b_re = bre_ref[...]
    b_im = bim_ref[...]
    bbre_ref[...] = coef_re * b_re - coef_im * b_im
    bbim_ref[...] = coef_re * b_im + coef_im * b_re


def _s5_prep(a_re, a_im, log_dt, b_re, b_im):
    g, p = a_re.shape
    bt_re = jnp.transpose(b_re, (0, 2, 1))
    bt_im = jnp.transpose(b_im, (0, 2, 1))
    i = bt_re.shape[1]
    sds = jax.ShapeDtypeStruct
    return pl.pallas_call(
        _s5_prep_kernel,
        out_shape=[sds((g, 1, p), F32), sds((g, 1, p), F32), sds((g, i, p), F32), sds((g, i, p), F32)],
        name="s5_discretise",
    )(a_re.reshape(g, 1, p), a_im.reshape(g, 1, p), log_dt.reshape(g, 1, 1), bt_re, bt_im)


def _s5_scan_kernel(u_ref, bb_ref, cc_ref, d_ref, are_ref, aim_ref, s0_ref, y_ref, sf_ref, bu_ref, xb_ref, st_ref,
                    *, nb, tc, cin):
    t_step = pl.program_id(1)
    half = bu_ref.shape[2] // 2
    packed_rows = nb % BF16_ROWS == 0

    @pl.when(t_step == 0)
    def _():
        st_ref[...] = s0_ref[...]

    us = []
    for pk in range(2):
        u = u_ref[:, pk * cin:(pk + 1) * cin]
        us.append(u)
        bu_ref[pk] = jnp.dot(u, bb_ref[pk], preferred_element_type=F32)
    for pk in range(2):
        a_re = jnp.broadcast_to(are_ref[pk], (nb, half))
        a_im = jnp.broadcast_to(aim_ref[pk], (nb, half))
        x_re = st_ref[pk, :, 0:half]
        x_im = st_ref[pk, :, half:2 * half]
        for t in range(tc):
            rows = slice(t * nb, (t + 1) * nb)
            n_re = a_re * x_re - a_im * x_im + bu_ref[pk, rows, 0:half]
            n_im = a_re * x_im + a_im * x_re + bu_ref[pk, rows, half:2 * half]
            x_re, x_im = n_re, n_im
            if packed_rows:
                xb_ref[pk, rows, 0:half] = x_re.astype(BF16)
                xb_ref[pk, rows, half:2 * half] = x_im.astype(BF16)
            else:
                bu_ref[pk, rows, 0:half] = x_re
                bu_ref[pk, rows, half:2 * half] = x_im
        st_ref[pk, :, 0:half] = x_re
        st_ref[pk, :, half:2 * half] = x_im
    sf_ref[...] = st_ref[...]
    for pk in range(2):
        xs = xb_ref[pk] if packed_rows else bu_ref[pk].astype(BF16)
        y = jnp.dot(xs, cc_ref[pk], preferred_element_type=F32)
        y = y + d_ref[pk] * us[pk].astype(F32)
        y_ref[:, pk * cin:(pk + 1) * cin] = jax.nn.gelu(y).astype(BF16)


def _s5_scan(u_tm, bb, cc, dvec, abar_re, abar_im, s0, *, nb, l):
    d_ssm = u_tm.shape[1]
    npk = bb.shape[0]
    cin = bb.shape[1]
    two_half = bb.shape[2]
    tc = min(SSM_TIME, l)
    rows = tc * nb
    assert npk % 2 == 0
    kern = functools.partial(_s5_scan_kernel, nb=nb, tc=tc, cin=cin)
    pair = lambda p, t: (p, 0, 0)
    return pl.pallas_call(
        kern,
        grid=(npk // 2, l // tc),
        in_specs=[pl.BlockSpec((rows, 2 * cin), lambda p, t: (t, p)),
                  pl.BlockSpec((2, cin, two_half), pair),
                  pl.BlockSpec((2, two_half, cin), pair),
                  pl.BlockSpec((2, 1, cin), pair),
                  pl.BlockSpec((2, 1, two_half // 2), pair),
                  pl.BlockSpec((2, 1, two_half // 2), pair),
                  pl.BlockSpec((2, nb, two_half), pair)],
        out_specs=[pl.BlockSpec((rows, 2 * cin), lambda p, t: (t, p)),
                   pl.BlockSpec((2, nb, two_half), pair)],
        out_shape=[jax.ShapeDtypeStruct((l * nb, d_ssm), BF16),
                   jax.ShapeDtypeStruct((npk, nb, two_half), F32)],
        scratch_shapes=[pltpu.VMEM((2, rows, two_half), F32), pltpu.VMEM((2, rows, two_half), BF16),
                        pltpu.VMEM((2, nb, two_half), F32)],
        compiler_params=_cparams(("arbitrary", "arbitrary"), VMEM_LIMIT),
        name="s5_scan",
    )(u_tm, bb, cc, dvec, abar_re, abar_im, s0)


def _s5_pack_weights(bb_re, bb_im, c_re, c_im, d, abar_re, abar_im):
    g, i, p = bb_re.shape
    npk = g // SSM_PACK
    eye = jnp.eye(SSM_PACK, dtype=F32)

    def blockdiag_in(m):
        m = m.reshape(npk, SSM_PACK, i, p)
        return jnp.einsum("ngip,gh->ngihp", m, eye).reshape(npk, SSM_PACK * i, SSM_PACK * p)

    def blockdiag_out(m):
        m = m.reshape(npk, SSM_PACK, i, p)
        return jnp.einsum("ngip,gh->ngphi", m, eye).reshape(npk, SSM_PACK * p, SSM_PACK * i)

    bb = jnp.concatenate([blockdiag_in(bb_re), blockdiag_in(bb_im)], axis=2).astype(BF16)
    cc = jnp.concatenate([blockdiag_out(c_re), -blockdiag_out(c_im)], axis=1).astype(BF16)
    dvec = d.reshape(npk, 1, SSM_PACK * i)
    are = abar_re.reshape(npk, 1, SSM_PACK * p)
    aim = abar_im.reshape(npk, 1, SSM_PACK * p)
    return bb, cc, dvec, are, aim


def _state_to_packs(s_re, s_im):
    b, g, p = s_re.shape
    npk = g // SSM_PACK
    re = jnp.transpose(s_re.reshape(b, npk, SSM_PACK * p), (1, 0, 2))
    im = jnp.transpose(s_im.reshape(b, npk, SSM_PACK * p), (1, 0, 2))
    return jnp.concatenate([re, im], axis=2)


def _packs_to_state(sf, g, p):
    npk, b, two_half = sf.shape
    half = two_half // 2
    re = jnp.transpose(sf[:, :, :half], (1, 0, 2)).reshape(b, g, p)
    im = jnp.transpose(sf[:, :, half:], (1, 0, 2)).reshape(b, g, p)
    return re, im


def _outproj_kernel(attn_ref, ssm_ref, x_ref, mod_ref, wglu_ref, bglu_ref, wout_ref, gpm_ref, gpf_ref,
                    wr_ref, br_ref, x1_ref, h2_ref, lg_ref):
    y = ssm_ref[...]
    z = jnp.dot(y, wglu_ref[...], preferred_element_type=F32) + bglu_ref[...]
    s = (y.astype(F32) * jax.nn.sigmoid(z)).astype(BF16)
    m = jnp.dot(attn_ref[0], wout_ref[0:D_ATTN, :], preferred_element_type=F32)
    m = m + jnp.dot(s, wout_ref[D_ATTN:, :], preferred_element_type=F32)
    x1 = x_ref[0] + mod_ref[0, 2:3, :] * _rms(m, gpm_ref[...])
    x1_ref[0] = x1
    h2 = (_rms(x1, gpf_ref[...]) * (1.0 + mod_ref[0, 4:5, :]) + mod_ref[0, 3:4, :]).astype(BF16)
    h2_ref[...] = h2
    lg_ref[...] = jnp.dot(h2, wr_ref[...], preferred_element_type=F32) + br_ref[...]


def _outproj(attn, ssm2d, x, mod, w_glu_bf, b_glu, w_out_bf, g_post_mix, g_pre_ffn, w_router_bf, b_router):
    b, l, d = x.shape
    tm = min(ROW_TILE, l)
    nt = l // tm
    d_ssm = w_glu_bf.shape[0]
    ne = w_router_bf.shape[1]
    c2 = lambda bi, j: (0, 0)
    return pl.pallas_call(
        _outproj_kernel,
        grid=(b, nt),
        in_specs=[pl.BlockSpec((1, tm, D_ATTN), lambda bi, j: (bi, j, 0)),
                  pl.BlockSpec((tm, d_ssm), lambda bi, j: (j, bi)),
                  pl.BlockSpec((1, tm, d), lambda bi, j: (bi, j, 0)),
                  pl.BlockSpec((1, 6, d), lambda bi, j: (bi, 0, 0)),
                  pl.BlockSpec(w_glu_bf.shape, c2),
                  pl.BlockSpec((1, d_ssm), c2),
                  pl.BlockSpec(w_out_bf.shape, c2),
                  pl.BlockSpec((1, d), c2),
                  pl.BlockSpec((1, d), c2),
                  pl.BlockSpec(w_router_bf.shape, c2),
                  pl.BlockSpec((1, ne), c2)],
        out_specs=[pl.BlockSpec((1, tm, d), lambda bi, j: (bi, j, 0)),
                   pl.BlockSpec((tm, d), lambda bi, j: (bi * nt + j, 0)),
                   pl.BlockSpec((tm, ne), lambda bi, j: (bi * nt + j, 0))],
        out_shape=[jax.ShapeDtypeStruct((b, l, d), F32),
                   jax.ShapeDtypeStruct((b * l, d), BF16),
                   jax.ShapeDtypeStruct((b * l, ne), F32)],
        compiler_params=_cparams(("arbitrary", "arbitrary"), VMEM_LIMIT),
        name="glu_outproj_norms_router",
    )(attn, ssm2d, x, mod, w_glu_bf, b_glu.reshape(1, d_ssm), w_out_bf, g_post_mix.reshape(1, d),
      g_pre_ffn.reshape(1, d), w_router_bf, b_router.reshape(1, ne))


def _router_kernel(lg_ref, tri_ref, info_ref, pc_ref):
    lg = lg_ref[...]
    ne, tb = lg.shape
    eidx = lax.broadcasted_iota(I32, (ne, tb), 0).astype(F32)
    work = lg
    sels, vals = [], []
    for _ in range(TOP_K):
        m = work.max(axis=0, keepdims=True)
        idx = jnp.where(work == m, eidx, float(ne)).min(axis=0, keepdims=True)
        sel = eidx == idx
        sels.append(sel)
        vals.append(m)
        work = jnp.where(sel, -jnp.inf, work)
    ex = [jnp.exp(v - vals[0]) for v in vals]
    den = ex[0] + ex[1] + ex[2] + ex[3]
    onehot = (sels[0] | sels[1] | sels[2] | sels[3]).astype(F32)
    cnt = onehot.sum(axis=1, keepdims=True).astype(I32)
    pc = ((cnt + (BF16_ROWS - 1)) >> 4) << 4
    pc_ref[0] = pc
    rank = jnp.dot(onehot.astype(BF16), tri_ref[...], preferred_element_type=F32)
    low = jnp.where(lax.broadcasted_iota(I32, (ne, LANES), 1) < lax.broadcasted_iota(I32, (ne, LANES), 0), 1.0, 0.0)
    pcb = jnp.concatenate([jnp.broadcast_to(pc.astype(F32), (ne, LANES)), jnp.zeros((LANES - ne, LANES), F32)], axis=0)
    off = jnp.dot(low.astype(BF16), pcb.astype(BF16), preferred_element_type=F32)[:, 0:1]
    posmat = rank + off
    for k in range(TOP_K):
        info_ref[k:k + 1, :] = jnp.where(sels[k], posmat, 0.0).sum(axis=0, keepdims=True)
        info_ref[TOP_K + k:TOP_K + k + 1, :] = ex[k] / den


def _router(logits_t, tb):
    ne, t = logits_t.shape
    nb = t // tb
    tri = jnp.triu(jnp.ones((tb, tb), BF16), k=1)
    return pl.pallas_call(
        _router_kernel,
        grid=(nb,),
        in_specs=[pl.BlockSpec((ne, tb), lambda i: (0, i)),
                  pl.BlockSpec((tb, tb), lambda i: (0, 0))],
        out_specs=[pl.BlockSpec((2 * TOP_K, tb), lambda i: (0, i)),
                   pl.BlockSpec((1, ne, 1), lambda i: (i, 0, 0))],
        out_shape=[jax.ShapeDtypeStruct((2 * TOP_K, t), F32),
                   jax.ShapeDtypeStruct((nb, ne, 1), I32)],
        compiler_params=_cparams(("arbitrary",)),
        name="router_top4",
    )(logits_t, tri)


def _local_chunks(tb, ne):
    rows = TOP_K * tb + ne * (BF16_ROWS - 1)
    return -(-rows // SORT_ROWS) * (SORT_ROWS // BF16_ROWS)


def _sort_plan(pc, tb, n_tokens):
    nb, ne = pc.shape
    nch = _local_chunks(tb, ne)
    n_tiles = (TOP_K * n_tokens + nb * ne * (BF16_ROWS - 1)) // FFN_TILE + ne
    loc_end = jnp.cumsum(pc, axis=1)
    loc_off = loc_end - pc
    tot = pc.sum(axis=0)
    tot_pad = ((tot + FFN_TILE - 1) // FFN_TILE) * FFN_TILE
    base_end = jnp.cumsum(tot_pad)
    base = base_end - tot_pad
    glob = base[None, :] + jnp.cumsum(pc, axis=0) - pc
    n_used = (base_end[-1] // FFN_TILE).astype(I32)
    tile_row = jnp.arange(n_tiles, dtype=I32) * FFN_TILE
    tile_expert = jnp.minimum((base_end[None, :] <= tile_row[:, None]).sum(axis=1), ne - 1).astype(I32)
    r = jnp.arange(nch, dtype=I32) * BF16_ROWS
    e_of = jnp.minimum((loc_end[:, None, :] <= r[None, :, None]).sum(axis=2), ne - 1)
    shift = glob - loc_off
    owner = e_of[:, :, None] == jnp.arange(ne, dtype=I32)[None, None, :]
    dst = jnp.where(owner, shift[:, None, :], 0).sum(axis=2) + r[None, :]
    n_valid = (loc_end[:, -1] // BF16_ROWS).astype(I32)
    valid = jnp.arange(nch, dtype=I32)[None, :] < n_valid[:, None]
    spare = n_tiles * (FFN_TILE // BF16_ROWS) + (jnp.arange(nb, dtype=I32)[:, None] % 2) * nch \
        + jnp.arange(nch, dtype=I32)[None, :]
    chunk_dst = jnp.where(valid, dst // BF16_ROWS, spare).astype(I32)
    chunk_src = jnp.where(valid, dst // BF16_ROWS, 0).astype(I32)
    tail_n = ((tot_pad - tot) // BF16_ROWS).astype(I32)
    tail_dst = ((base + tot) // BF16_ROWS).astype(I32)
    spare_tiles = -(-2 * nch * BF16_ROWS // FFN_TILE)
    return dict(nch=nch, n_tiles=n_tiles, spare_tiles=spare_tiles, chunk_dst=chunk_dst.reshape(-1),
                chunk_src=chunk_src.reshape(-1), n_used=n_used, tile_expert=tile_expert, tail_n=tail_n,
                tail_dst=tail_dst)


def _chunk_copy(buf, hbm, slot, c, dst_chunk, sem, to_hbm):
    v = buf.at[slot, pl.ds(pl.multiple_of(c * BF16_ROWS, BF16_ROWS), BF16_ROWS)]
    h = hbm.at[pl.ds(pl.multiple_of(dst_chunk * BF16_ROWS, BF16_ROWS), BF16_ROWS)]
    return pltpu.make_async_copy(v, h, sem) if to_hbm else pltpu.make_async_copy(h, v, sem)


def _local_row_ids(rt, tb):
    return lax.broadcasted_iota(I32, (rt, tb), 0).astype(F32).astype(BF16)


def _local_pos(pos_row, r0, rt):
    return jnp.clip(pos_row - float(r0), -1.0, float(rt)).astype(BF16)


def _dispatch_kernel(cd_ref, tn_ref, td_ref, nu_ref, h_ref, info_ref, hx_ref, infox_ref, xs_ref,
                     buf_ref, zero_ref, ztile_ref, sem, zsem, *, nch, rt, n_tiles, spare_tiles):
    b = pl.program_id(0)
    nb = pl.num_programs(0)
    slot = b % 2

    def sort_block(hr, ir):
        tb = hr.shape[0]
        h = hr[...]
        rid = _local_row_ids(rt, tb)
        one = jnp.ones((rt, tb), BF16)
        for r0 in range(0, _local_chunks(tb, N_EXPERTS) * BF16_ROWS, rt):
            p = jnp.zeros((rt, tb), BF16)
            for k in range(TOP_K):
                p = jnp.where(rid == _local_pos(ir[k:k + 1, :], r0, rt), one, p)
            buf_ref[slot, r0:r0 + rt, :] = jnp.dot(p, h, preferred_element_type=F32).astype(BF16)

    pl.when(b < nb - 1)(lambda: sort_block(h_ref, info_ref))
    pl.when(b == nb - 1)(lambda: sort_block(hx_ref, infox_ref))

    for c in range(nch):
        _chunk_copy(buf_ref, xs_ref, slot, c, cd_ref[b * nch + c], sem.at[slot], True).start()

    def drain(step, sl):
        for c in range(nch):
            _chunk_copy(buf_ref, xs_ref, sl, c, cd_ref[step * nch + c], sem.at[sl], True).wait()

    @pl.when(b > 0)
    def _():
        drain(b - 1, 1 - slot)

    @pl.when(b == nb - 1)
    def _():
        drain(b, slot)
        zero_ref[...] = jnp.zeros_like(zero_ref)
        for e in range(N_EXPERTS):
            def zcopy(c, e=e):
                dst = pl.multiple_of((td_ref[e] + c) * BF16_ROWS, BF16_ROWS)
                return pltpu.make_async_copy(zero_ref, xs_ref.at[pl.ds(dst, BF16_ROWS)], zsem)

            lax.fori_loop(0, tn_ref[e], lambda c, _: (zcopy(c).start(), 0)[1], 0)
            lax.fori_loop(0, tn_ref[e], lambda c, _: (zcopy(c).wait(), 0)[1], 0)

        ztile_ref[...] = jnp.zeros_like(ztile_ref)

        def ztile(i):
            dst = pl.multiple_of(i * FFN_TILE, FFN_TILE)
            return pltpu.make_async_copy(ztile_ref, xs_ref.at[pl.ds(dst, FFN_TILE)], zsem)

        lax.fori_loop(nu_ref[0], n_tiles + spare_tiles, lambda i, _: (ztile(i).start(), 0)[1], 0)
        lax.fori_loop(nu_ref[0], n_tiles + spare_tiles, lambda i, _: (ztile(i).wait(), 0)[1], 0)


def _dispatch(h2, info, h2x, infox, plan, tb):
    t, d = h2.shape
    tx = h2x.shape[0]
    nb = t // tb
    nch = plan["nch"]
    rows = (plan["n_tiles"] + plan["spare_tiles"]) * FFN_TILE
    assert (nch * BF16_ROWS) % SORT_ROWS == 0 and tx <= tb
    kern = functools.partial(_dispatch_kernel, nch=nch, rt=SORT_ROWS, n_tiles=plan["n_tiles"],
                             spare_tiles=plan["spare_tiles"])
    grid_spec = pltpu.PrefetchScalarGridSpec(
        num_scalar_prefetch=4,
        grid=(nb + 1,),
        in_specs=[pl.BlockSpec((tb, d), lambda i, *_: (jnp.minimum(i, nb - 1), 0)),
                  pl.BlockSpec((2 * TOP_K, tb), lambda i, *_: (0, jnp.minimum(i, nb - 1))),
                  pl.BlockSpec((tx, d), lambda i, *_: (0, 0)),
                  pl.BlockSpec((2 * TOP_K, tx), lambda i, *_: (0, 0))],
        out_specs=pl.BlockSpec(memory_space=pl.ANY),
        scratch_shapes=[pltpu.VMEM((2, nch * BF16_ROWS, d), BF16),
                        pltpu.VMEM((BF16_ROWS, d), BF16),
                        pltpu.VMEM((FFN_TILE, d), BF16),
                        pltpu.SemaphoreType.DMA((2,)),
                        pltpu.SemaphoreType.DMA],
    )
    return pl.pallas_call(
        kern,
        grid_spec=grid_spec,
        out_shape=jax.ShapeDtypeStruct((rows, d), BF16),
        compiler_params=_cparams(("arbitrary",), VMEM_LIMIT),
        name="moe_dispatch",
    )(plan["chunk_dst"], plan["tail_n"], plan["tail_dst"], plan["n_used"].reshape(1), h2, info, h2x, infox)


def _ffn_kernel(te_ref, nu_ref, x_ref, wgu_ref, bgu_ref, wd_ref, bd_ref, perm_ref, y_ref, wgu_s, wd_s):
    i = pl.program_id(0)
    f = wd_s.shape[0]
    prev = te_ref[jnp.maximum(i - 1, 0)]
    new_expert = (i == 0) | (te_ref[i] != prev)

    @pl.when((i < nu_ref[0]) & new_expert)
    def _():
        for c in range(0, 2 * f, 2 * LANES):
            blk = wgu_ref[0, :, c:c + 2 * LANES].astype(BF16)
            wgu_s[:, c:c + 2 * LANES] = jnp.dot(blk, perm_ref[...], preferred_element_type=F32).astype(BF16)
        wd_s[...] = wd_ref[0].astype(BF16)

    @pl.when(i < nu_ref[0])
    def _():
        x = x_ref[...]
        acts = []
        for c in range(0, 2 * f, 2 * LANES):
            gu = jnp.dot(x, wgu_s[:, c:c + 2 * LANES], preferred_element_type=F32) + bgu_ref[0, :, c:c + 2 * LANES]
            glu = jnp.minimum(gu[:, :LANES], SWIGLU_LIMIT)
            lin = jnp.clip(gu[:, LANES:], -SWIGLU_LIMIT, SWIGLU_LIMIT)
            acts.append((glu * jax.nn.sigmoid(SWIGLU_ALPHA * glu) * (lin + 1.0)).astype(BF16))
        act = jnp.concatenate(acts, axis=1)
        y_ref[...] = (jnp.dot(act, wd_s[...], preferred_element_type=F32) + bd_ref[0]).astype(BF16)

    @pl.when(i >= nu_ref[0])
    def _():
        y_ref[...] = jnp.zeros_like(y_ref)


def _ffn(xs, plan, w_gate_up, b_gu_perm, w_down, b_down):
    d = xs.shape[1]
    ne, _, f2 = w_gate_up.shape
    f = f2 // 2
    n_tiles = plan["n_tiles"]
    rows = n_tiles * FFN_TILE
    src = jnp.concatenate([jnp.arange(LANES) * 2, jnp.arange(LANES) * 2 + 1])
    perm = (jnp.arange(2 * LANES)[:, None] == src[None, :]).astype(BF16)

    def xi(i, te, nu):
        return (jnp.minimum(i, nu[0] - 1), 0)

    def wi(i, te, nu):
        return (te[jnp.minimum(i, nu[0] - 1)], 0, 0)

    grid_spec = pltpu.PrefetchScalarGridSpec(
        num_scalar_prefetch=2,
        grid=(n_tiles,),
        in_specs=[pl.BlockSpec((FFN_TILE, d), xi),
                  pl.BlockSpec((1, d, f2), wi),
                  pl.BlockSpec((1, 1, f2), wi),
                  pl.BlockSpec((1, f, d), wi),
                  pl.BlockSpec((1, 1, d), wi),
                  pl.BlockSpec((2 * LANES, 2 * LANES), lambda i, te, nu: (0, 0))],
        out_specs=pl.BlockSpec((FFN_TILE, d), lambda i, te, nu: (i, 0)),
        scratch_shapes=[pltpu.VMEM((d, f2), BF16), pltpu.VMEM((f, d), BF16)],
    )
    return pl.pallas_call(
        _ffn_kernel,
        grid_spec=grid_spec,
        out_shape=jax.ShapeDtypeStruct((rows, d), BF16),
        compiler_params=_cparams(("arbitrary",), VMEM_LIMIT),
        name="moe_expert_ffn",
    )(plan["tile_expert"], plan["n_used"].reshape(1), xs, w_gate_up, b_gu_perm.reshape(ne, 1, f2), w_down,
      b_down.reshape(ne, 1, d), perm)


def _combine_kernel(cd_ref, ys_ref, info_ref, x1_ref, mod_ref, infox_ref, x1x_ref, modx_ref, g_ref,
                    o_ref, ox_ref, buf_ref, w_ref, sem, *, nch, rt):
    b = pl.program_id(0)
    nb = pl.num_programs(0)

    for c in range(nch):
        _chunk_copy(buf_ref, ys_ref, 0, c, cd_ref[b * nch + c], sem, False).start()

    def unsort_block(ir, x1r, modr, outr):
        tb = x1r.shape[0]
        nbat = modr.shape[0]
        rb = tb // nbat
        nrows = _local_chunks(tb, N_EXPERTS) * BF16_ROWS
        rid = _local_row_ids(rt, tb)
        for r0 in range(0, nrows, rt):
            w = jnp.zeros((rt, tb), BF16)
            for k in range(TOP_K):
                gate = jnp.broadcast_to(ir[TOP_K + k:TOP_K + k + 1, :].astype(BF16), (rt, tb))
                w = jnp.where(rid == _local_pos(ir[k:k + 1, :], r0, rt), gate, w)
            w_ref[r0:r0 + rt, 0:tb] = w

        for c in range(nch):
            _chunk_copy(buf_ref, ys_ref, 0, c, cd_ref[b * nch + c], sem, False).wait()

        moe = lax.dot_general(w_ref[0:nrows, 0:tb], buf_ref[0, 0:nrows, :], (((0,), (0,)), ((), ())),
                              preferred_element_type=F32)
        out = _rms(moe, g_ref[...])
        for s in range(nbat):
            rs = slice(s * rb, (s + 1) * rb)
            outr[rs, :] = x1r[rs, :] + modr[s, 5:6, :] * out[rs, :]

    pl.when(b < nb - 1)(lambda: unsort_block(info_ref, x1_ref, mod_ref, o_ref))
    pl.when(b == nb - 1)(lambda: unsort_block(infox_ref, x1x_ref, modx_ref, ox_ref))


def _combine(ys, info, x1, mod, infox, x1x, modx, g_post_ffn, plan, tb, l):
    t, d = x1.shape
    tx = x1x.shape[0]
    nb = t // tb
    nch = plan["nch"]
    assert l % tb == 0
    clamp = lambda i: jnp.minimum(i, nb - 1)
    kern = functools.partial(_combine_kernel, nch=nch, rt=SORT_ROWS)
    grid_spec = pltpu.PrefetchScalarGridSpec(
        num_scalar_prefetch=1,
        grid=(nb + 1,),
        in_specs=[pl.BlockSpec(memory_space=pl.ANY),
                  pl.BlockSpec((2 * TOP_K, tb), lambda i, *_: (0, clamp(i))),
                  pl.BlockSpec((tb, d), lambda i, *_: (clamp(i), 0)),
                  pl.BlockSpec((1, 6, d), lambda i, *_: (clamp(i) // (l // tb), 0, 0)),
                  pl.BlockSpec((2 * TOP_K, tx), lambda i, *_: (0, 0)),
                  pl.BlockSpec((tx, d), lambda i, *_: (0, 0)),
                  pl.BlockSpec(modx.shape, lambda i, *_: (0, 0, 0)),
                  pl.BlockSpec((1, d), lambda i, *_: (0, 0))],
        out_specs=[pl.BlockSpec((tb, d), lambda i, *_: (clamp(i), 0)),
                   pl.BlockSpec((tx, d), lambda i, *_: (0, 0))],
        scratch_shapes=[pltpu.VMEM((1, nch * BF16_ROWS, d), BF16), pltpu.VMEM((nch * BF16_ROWS, tb), BF16),
                        pltpu.SemaphoreType.DMA],
    )
    return pl.pallas_call(
        kern,
        grid_spec=grid_spec,
        out_shape=[jax.ShapeDtypeStruct((t, d), F32), jax.ShapeDtypeStruct((tx, d), F32)],
        compiler_params=_cparams(("arbitrary",), VMEM_LIMIT),
        name="moe_combine",
    )(plan["chunk_src"], ys, info, x1, mod, infox, x1x, modx, g_post_ffn.reshape(1, d))


def _moe(main, second, l, g_post_ffn, w_gate_up, b_gu_perm, w_down, b_down):
    h2, logits, x1, mod = main
    h2x, logitsx, x1x, modx = second
    t = h2.shape[0]
    tx = h2x.shape[0]
    tb = ROW_TILE
    info, pc = _router(logits[:, :N_EXPERTS].T, tb)
    infox, pcx = _router(logitsx[:, :N_EXPERTS].T, tx)
    pc_all = jnp.concatenate([pc.reshape(t // tb, N_EXPERTS), pcx.reshape(1, N_EXPERTS)], axis=0)
    plan = _sort_plan(pc_all, tb, t + tx)
    xs = _dispatch(h2, info, h2x, infox, plan, tb)
    ys = _ffn(xs, plan, w_gate_up, b_gu_perm, w_down, b_down)
    return _combine(ys, info, x1, mod, infox, x1x, modx, g_post_ffn, plan, tb, l)


def _stream(x, mod, wts, attn_fn, s0_packs):
    b, l, d = x.shape
    q, k_att, v_att, u2d, kf, vf = _inproj(x, mod, wts["g_pre_mix"], wts["w_in"], pad_front=attn_fn is None)
    if attn_fn is None:
        attn = _attn_prompt(q, k_att, v_att, wts["table_prompt"])
    else:
        attn = attn_fn(q, k_att, v_att)
    d_ssm = u2d.shape[1] // b
    u_tm = u2d.reshape(l * b, d_ssm)
    y_tm, sf = _s5_scan(u_tm, wts["bb"], wts["cc"], wts["dvec"], wts["are"], wts["aim"], s0_packs, nb=b, l=l)
    ssm2d = y_tm.reshape(l, b * d_ssm)
    x1, h2, logits = _outproj(attn, ssm2d, x, mod, wts["w_glu"], wts["b_glu"], wts["w_out"], wts["g_post_mix"],
                              wts["g_pre_ffn"], wts["w_router"], wts["b_router"])
    return (h2, logits, x1.reshape(b * l, d), mod), kf, vf, sf


def kernel(x_prompt, x_sample, cache_k, cache_v, state_ssm_re, state_ssm_im, c_prompt, c_sample, w_ada, b_ada, g_pre_mix, g_post_mix, g_pre_ffn, g_post_ffn, w_in, rel_bias, ssm_a_re, ssm_a_im, ssm_log_dt, ssm_b_re, ssm_b_im, ssm_c_re, ssm_c_im, ssm_d, w_glu, b_glu, w_out, w_router, b_router, w_gate_up, b_gate_up, w_down, b_down):
    depth = w_ada.shape[0]
    assert depth == 1, "single-layer trunk"
    bp, lp, d = x_prompt.shape
    bs, ls, _ = x_sample.shape
    n_cache = cache_k.shape[2]
    past_len = PAST_LEN
    g, p = ssm_a_re.shape[1:]
    lyr = 0

    mod = _ada(jnp.concatenate([c_prompt, c_sample], axis=0), w_ada[lyr], b_ada[lyr])
    mod = mod.reshape(bp + bs, 6, d)

    abre, abim, bbre, bbim = _s5_prep(ssm_a_re[lyr], ssm_a_im[lyr], ssm_log_dt[lyr], ssm_b_re[lyr], ssm_b_im[lyr])
    bb, cc, dvec, are, aim = _s5_pack_weights(bbre, bbim, ssm_c_re[lyr], ssm_c_im[lyr], ssm_d[lyr], abre, abim)

    f2 = w_gate_up.shape[-1]
    b_gu_perm = b_gate_up[lyr].reshape(N_EXPERTS, f2 // (2 * LANES), LANES, 2)
    b_gu_perm = jnp.transpose(b_gu_perm, (0, 1, 3, 2)).reshape(N_EXPERTS, f2)

    rb = rel_bias[lyr]
    wts = dict(
        g_pre_mix=g_pre_mix[lyr], g_post_mix=g_post_mix[lyr], g_pre_ffn=g_pre_ffn[lyr], g_post_ffn=g_post_ffn[lyr],
        w_in=w_in[lyr].astype(BF16), w_glu=w_glu[lyr].astype(BF16), b_glu=b_glu[lyr], w_out=w_out[lyr].astype(BF16),
        w_router=jnp.pad(w_router[lyr], ((0, 0), (0, LANES - N_EXPERTS))).astype(BF16),
        b_router=jnp.pad(b_router[lyr], (0, LANES - N_EXPERTS)),
        w_gate_up=w_gate_up[lyr], b_gu_perm=b_gu_perm, w_down=w_down[lyr], b_down=b_down[lyr],
        bb=bb, cc=cc, dvec=dvec, are=are, aim=aim,
        table_prompt=_bias_table(rb, q0=WIN_ROWS, k0=0, nq=CHUNK, nk=Q_TILE + WIN_ROWS, strips=Q_TILE // CHUNK),
    )

    zero_state = jnp.zeros((g // SSM_PACK, bp, 2 * SSM_PACK * p), F32)
    ffn_in_p, kf_p, vf_p, sf_p = _stream(x_prompt, mod[:bp], wts, None, zero_state)

    tabc = _bias_table(rb, q0=past_len, k0=past_len - n_cache, nq=ls, nk=n_cache)
    tabn = _bias_table(rb, q0=past_len, k0=past_len, nq=ls, nk=ls)
    ck = cache_k[lyr].reshape(bs, n_cache, D_ATTN)
    cv = cache_v[lyr].reshape(bs, n_cache, D_ATTN)
    attn_s = lambda q, k, v: _attn_sample(q, k, v, ck, cv, tabc, tabn)
    s0 = _state_to_packs(state_ssm_re[lyr], state_ssm_im[lyr])
    ffn_in_s, kf_s, vf_s, sf_s = _stream(x_sample, mod[bp:], wts, attn_s, s0)

    y_p, y_s = _moe(ffn_in_p, ffn_in_s, lp, wts["g_post_ffn"], wts["w_gate_up"], wts["b_gu_perm"], wts["w_down"],
                    wts["b_down"])
    y_p = y_p.reshape(bp, lp, d)
    y_s = y_s.reshape(bs, ls, d)

    def heads(a):
        return a.reshape(1, a.shape[0], a.shape[1], N_HEADS, HEAD_DIM)

    sre_p, sim_p = _packs_to_state(sf_p, g, p)
    sre_s, sim_s = _packs_to_state(sf_s, g, p)
    return (y_p, y_s, heads(kf_p), heads(vf_p), sre_p[None], sim_p[None],
            heads(kf_s), heads(vf_s), sre_s[None], sim_s[None])
```

```python
import functools
import math

import jax
import jax.numpy as jnp
from jax import lax
from jax.experimental import pallas as pl
from jax.experimental.pallas import tpu as pltpu

F32 = jnp.float32
BF16 = jnp.bfloat16
I32 = jnp.int32
HIGHEST = lax.Precision.HIGHEST

CHUNK = 64
CHUNK_LOG2 = 6
N_PAST_CHUNKS = 8
PAST_LEN = 2048
WIN_ROWS = N_PAST_CHUNKS * CHUNK
MAX_REL = 64
N_REL = 2 * MAX_REL + 1
N_HEADS = 8
HEAD_DIM = 64
D_ATTN = N_HEADS * HEAD_DIM
SSM_GROUP = 16
SSM_STATE = 64
N_EXPERTS = 32
TOP_K = 4
SWIGLU_LIMIT = 7.0
SWIGLU_ALPHA = 1.702
RMS_EPS = 1e-6
NEG_INF = -1e30
LOG2E = math.log2(math.e)

LANES = 128
BF16_ROWS = 16
VMEM_LIMIT = 56 * 1024 * 1024

ROW_TILE = 512
Q_TILE = 256
ATTN_QSUB = 128
ATTN_KBLK = 256
SSM_PACK = 8
SSM_TIME = 64
FFN_TILE = 512
SORT_ROWS = 256


def _cparams(sem, vmem=None):
    return pltpu.CompilerParams(dimension_semantics=sem, vmem_limit_bytes=vmem)


def _rms(x, g):
    return x * lax.rsqrt(jnp.mean(x * x, axis=-1, keepdims=True) + RMS_EPS) * g


def _ada_kernel(c_ref, w_ref, b_ref, o_ref):
    c = c_ref[...]
    s = c * jax.nn.sigmoid(c)
    o_ref[...] = jnp.dot(s, w_ref[...], precision=HIGHEST, preferred_element_type=F32) + b_ref[...]


def _ada(c, w_ada, b_ada):
    n, d = c.shape
    nout = w_ada.shape[1]
    return pl.pallas_call(
        _ada_kernel,
        grid=(nout // d,),
        in_specs=[pl.BlockSpec((n, d), lambda j: (0, 0)),
                  pl.BlockSpec((d, d), lambda j: (0, j)),
                  pl.BlockSpec((1, d), lambda j: (0, j))],
        out_specs=pl.BlockSpec((n, d), lambda j: (0, j)),
        out_shape=jax.ShapeDtypeStruct((n, nout), F32),
        compiler_params=_cparams(("arbitrary",), VMEM_LIMIT),
        name="ada_mod",
    )(c, w_ada, b_ada.reshape(1, nout))


def _inproj_kernel(x_ref, mod_ref, g_ref, w_ref, q_ref, k_ref, v_ref, u_ref, kf_ref, vf_ref, *, first_keep, pad):
    j = pl.program_id(1)

    def compute():
        x = x_ref[0]
        h = _rms(x, g_ref[...]) * (1.0 + mod_ref[0, 1:2, :]) + mod_ref[0, 0:1, :]
        proj = jnp.dot(h.astype(BF16), w_ref[...], preferred_element_type=F32)
        q_ref[0] = (proj[:, :D_ATTN] * (HEAD_DIM ** -0.5 * LOG2E)).astype(BF16)
        k_ref[0] = proj[:, D_ATTN:2 * D_ATTN].astype(BF16)
        v_ref[0] = proj[:, 2 * D_ATTN:3 * D_ATTN].astype(BF16)
        u_ref[...] = proj[:, 3 * D_ATTN:].astype(BF16)

        @pl.when(j >= first_keep + pad)
        def _():
            kf_ref[0] = proj[:, D_ATTN:2 * D_ATTN]
            vf_ref[0] = proj[:, 2 * D_ATTN:3 * D_ATTN]

    if pad:
        @pl.when(j == 0)
        def _():
            k_ref[...] = jnp.zeros_like(k_ref)
            v_ref[...] = jnp.zeros_like(v_ref)

        pl.when(j > 0)(compute)
    else:
        compute()


def _inproj(x, mod, g, w_in_bf, *, pad_front):
    b, l, d = x.shape
    tm = min(ROW_TILE, l)
    nt = l // tm
    keep = min(WIN_ROWS, l)
    first_keep = (l - keep) // tm
    pad = 1 if pad_front else 0
    d_ssm = w_in_bf.shape[1] - 3 * D_ATTN
    assert l % tm == 0 and keep % tm == 0 and (not pad_front or tm == WIN_ROWS)

    def xi(bi, j):
        return (bi, jnp.maximum(j - pad, 0), 0)

    def kfi(bi, j):
        return (bi, jnp.maximum(j - pad - first_keep, 0), 0)

    kern = functools.partial(_inproj_kernel, first_keep=first_keep, pad=pad)
    return pl.pallas_call(
        kern,
        grid=(b, nt + pad),
        in_specs=[pl.BlockSpec((1, tm, d), xi),
                  pl.BlockSpec((1, 6, d), lambda bi, j: (bi, 0, 0)),
                  pl.BlockSpec((1, d), lambda bi, j: (0, 0)),
                  pl.BlockSpec(w_in_bf.shape, lambda bi, j: (0, 0))],
        out_specs=[pl.BlockSpec((1, tm, D_ATTN), xi),
                   pl.BlockSpec((1, tm, D_ATTN), lambda bi, j: (bi, j, 0)),
                   pl.BlockSpec((1, tm, D_ATTN), lambda bi, j: (bi, j, 0)),
                   pl.BlockSpec((tm, d_ssm), lambda bi, j: (jnp.maximum(j - pad, 0), bi)),
                   pl.BlockSpec((1, tm, D_ATTN), kfi),
                   pl.BlockSpec((1, tm, D_ATTN), kfi)],
        out_shape=[jax.ShapeDtypeStruct((b, l, D_ATTN), BF16),
                   jax.ShapeDtypeStruct((b, l + pad * tm, D_ATTN), BF16),
                   jax.ShapeDtypeStruct((b, l + pad * tm, D_ATTN), BF16),
                   jax.ShapeDtypeStruct((l, b * d_ssm), BF16),
                   jax.ShapeDtypeStruct((b, keep, D_ATTN), F32),
                   jax.ShapeDtypeStruct((b, keep, D_ATTN), F32)],
        compiler_params=_cparams(("arbitrary", "arbitrary"), VMEM_LIMIT),
        name="prenorm_inproj",
    )(x, mod, g.reshape(1, d), w_in_bf)


def _bias_kernel(rb_ref, o_ref, *, q0, k0, nq, nk, strips):
    h = pl.program_id(0)
    qpos = q0 + lax.broadcasted_iota(I32, (nq, nk), 0)
    kpos = k0 + lax.broadcasted_iota(I32, (nq, nk), 1)
    rel = jnp.clip(kpos - qpos, -MAX_REL, MAX_REL) + MAX_REL
    qc = qpos >> CHUNK_LOG2
    kc = kpos >> CHUNK_LOG2
    allowed = (kc <= qc) & (kc >= qc - N_PAST_CHUNKS)

    def body(r, acc):
        return jnp.where(rel == r, rb_ref[r, h], acc)

    val = lax.fori_loop(0, N_REL, body, jnp.zeros((nq, nk), F32))
    strip = jnp.where(allowed, val * LOG2E, NEG_INF)
    o_ref[0, 0:nq, :] = strip
    for c in range(1, strips):
        o_ref[0, c * nq:(c + 1) * nq, :] = pltpu.roll(strip, c * CHUNK, axis=1)


def _bias_table(rel_bias, *, q0, k0, nq, nk, strips=1):
    kern = functools.partial(_bias_kernel, q0=q0, k0=k0, nq=nq, nk=nk, strips=strips)
    return pl.pallas_call(
        kern,
        grid=(N_HEADS,),
        in_specs=[pl.BlockSpec(memory_space=pltpu.SMEM)],
        out_specs=pl.BlockSpec((1, strips * nq, nk), lambda h: (h, 0, 0)),
        out_shape=jax.ShapeDtypeStruct((N_HEADS, strips * nq, nk), F32),
        compiler_params=_cparams(("arbitrary",)),
        name="bias_table",
    )(rel_bias)


def _softmax_pv(scores, values):
    m = scores[0].max(axis=-1, keepdims=True)
    for s in scores[1:]:
        m = jnp.maximum(m, s.max(axis=-1, keepdims=True))
    acc = None
    den = None
    for s, v in zip(scores, values):
        p = jnp.exp2(s - m)
        d = p.sum(axis=-1, keepdims=True)
        o = jnp.dot(p.astype(BF16), v, preferred_element_type=F32)
        acc = o if acc is None else acc + o
        den = d if den is None else den + d
    return acc / den


def _attn_prompt_kernel(q_ref, k_ref, v_ref, tab_ref, o_ref, *, tq, band):
    qi = pl.program_id(1)
    t0 = pl.multiple_of(qi * tq, tq)
    lane = lax.broadcasted_iota(I32, (1, LANES), 1)

    def body(mask_front):
        if mask_front:
            col = lax.broadcasted_iota(I32, (1, band), 1)
            colmask = jnp.where(col + t0 >= WIN_ROWS, 0.0, NEG_INF).astype(F32)
        for hp in range(N_HEADS // 2):
            ls = slice(hp * LANES, (hp + 1) * LANES)
            for q0 in range(0, tq, ATTN_QSUB):
                rows = slice(q0, q0 + ATTN_QSUB)
                q2 = q_ref[0, rows, ls]
                outs = []
                for a in range(2):
                    qa = jnp.where((lane < HEAD_DIM) == (a == 0), q2, jnp.zeros_like(q2))
                    m = den = acc = None
                    for k0 in range(0, band, ATTN_KBLK):
                        cols = slice(k0, k0 + ATTN_KBLK)
                        kk = k_ref[0, pl.ds(t0 + k0, ATTN_KBLK), ls]
                        vv = v_ref[0, pl.ds(t0 + k0, ATTN_KBLK), ls]
                        s = lax.dot_general(qa, kk, (((1,), (1,)), ((), ())), preferred_element_type=F32)
                        s = s + tab_ref[2 * hp + a, rows, cols]
                        if mask_front:
                            s = s + colmask[:, cols]
                        mb = s.max(axis=-1, keepdims=True)
                        if m is None:
                            m = mb
                            p = jnp.exp2(s - m)
                            den = p.sum(axis=-1, keepdims=True)
                            acc = jnp.dot(p.astype(BF16), vv, preferred_element_type=F32)
                        else:
                            m_new = jnp.maximum(m, mb)
                            alpha = jnp.exp2(m - m_new)
                            p = jnp.exp2(s - m_new)
                            den = alpha * den + p.sum(axis=-1, keepdims=True)
                            acc = alpha * acc + jnp.dot(p.astype(BF16), vv, preferred_element_type=F32)
                            m = m_new
                    outs.append(acc / den)
                o_ref[0, rows, ls] = jnp.where(lane < HEAD_DIM, outs[0], outs[1]).astype(BF16)

    pl.when(t0 < WIN_ROWS)(lambda: body(True))
    pl.when(t0 >= WIN_ROWS)(lambda: body(False))


def _attn_prompt(q, k_pad, v_pad, table):
    b, l, _ = q.shape
    tq = Q_TILE
    band = tq + WIN_ROWS
    kern = functools.partial(_attn_prompt_kernel, tq=tq, band=band)
    return pl.pallas_call(
        kern,
        grid=(b, l // tq),
        in_specs=[pl.BlockSpec((1, tq, D_ATTN), lambda bi, i: (bi, i, 0)),
                  pl.BlockSpec((1, l + WIN_ROWS, D_ATTN), lambda bi, i: (bi, 0, 0)),
                  pl.BlockSpec((1, l + WIN_ROWS, D_ATTN), lambda bi, i: (bi, 0, 0)),
                  pl.BlockSpec(table.shape, lambda bi, i: (0, 0, 0))],
        out_specs=pl.BlockSpec((1, tq, D_ATTN), lambda bi, i: (bi, i, 0)),
        out_shape=jax.ShapeDtypeStruct((b, l, D_ATTN), BF16),
        compiler_params=_cparams(("arbitrary", "arbitrary"), VMEM_LIMIT),
        name="attn_prompt",
    )(q, k_pad, v_pad, table)


def _attn_sample_kernel(q_ref, k_ref, v_ref, ck_ref, cv_ref, tabc_ref, tabn_ref, o_ref):
    lane = lax.broadcasted_iota(I32, (1, LANES), 1)
    for hp in range(N_HEADS // 2):
        ls = slice(hp * LANES, (hp + 1) * LANES)
        q2 = q_ref[0, :, ls]
        kn = k_ref[0, :, ls]
        vn = v_ref[0, :, ls]
        kc = ck_ref[0, :, ls].astype(BF16)
        vc = cv_ref[0, :, ls].astype(BF16)
        outs = []
        for a in range(2):
            h = 2 * hp + a
            qa = jnp.where((lane < HEAD_DIM) == (a == 0), q2, jnp.zeros_like(q2))
            sc = lax.dot_general(qa, kc, (((1,), (1,)), ((), ())), preferred_element_type=F32) + tabc_ref[h]
            sn = lax.dot_general(qa, kn, (((1,), (1,)), ((), ())), preferred_element_type=F32) + tabn_ref[h]
            outs.append(_softmax_pv([sc, sn], [vc, vn]))
        o_ref[0, :, ls] = jnp.where(lane < HEAD_DIM, outs[0], outs[1]).astype(BF16)


def _attn_sample(q, k, v, cache_k, cache_v, tabc, tabn):
    b, s, _ = q.shape
    nc = cache_k.shape[1]
    return pl.pallas_call(
        _attn_sample_kernel,
        grid=(b,),
        in_specs=[pl.BlockSpec((1, s, D_ATTN), lambda bi: (bi, 0, 0)),
                  pl.BlockSpec((1, s, D_ATTN), lambda bi: (bi, 0, 0)),
                  pl.BlockSpec((1, s, D_ATTN), lambda bi: (bi, 0, 0)),
                  pl.BlockSpec((1, nc, D_ATTN), lambda bi: (bi, 0, 0)),
                  pl.BlockSpec((1, nc, D_ATTN), lambda bi: (bi, 0, 0)),
                  pl.BlockSpec(tabc.shape, lambda bi: (0, 0, 0)),
                  pl.BlockSpec(tabn.shape, lambda bi: (0, 0, 0))],
        out_specs=pl.BlockSpec((1, s, D_ATTN), lambda bi: (bi, 0, 0)),
        out_shape=jax.ShapeDtypeStruct((b, s, D_ATTN), BF16),
        compiler_params=_cparams(("arbitrary",), VMEM_LIMIT),
        name="attn_sample",
    )(q, k, v, cache_k, cache_v, tabc, tabn)


def _s5_prep_kernel(are_ref, aim_ref, ldt_ref, bre_ref, bim_ref, abre_ref, abim_ref, bbre_ref, bbim_ref):
    a_re = are_ref[...]
    a_im = aim_ref[...]
    dt = jnp.exp(ldt_ref[...])
    mag = jnp.exp(dt * a_re)
    abar_re = mag * jnp.cos(dt * a_im)
    abar_im = mag * jnp.sin(dt * a_im)
    den = a_re * a_re + a_im * a_im
    num_re = abar_re - 1.0
    coef_re = (num_re * a_re + abar_im * a_im) / den
    coef_im = (abar_im * a_re - num_re * a_im) / den
    abre_ref[...] = abar_re
    abim_ref[...] = abar_im
    b_re = bre_ref[...]
    b_im = bim_ref[...]
    bbre_ref[...] = coef_re * b_re - coef_im * b_im
    bbim_ref[...] = coef_re * b_im + coef_im * b_re


def _s5_prep(a_re, a_im, log_dt, b_re, b_im):
    g, p = a_re.shape
    bt_re = jnp.transpose(b_re, (0, 2, 1))
    bt_im = jnp.transpose(b_im, (0, 2, 1))
    i = bt_re.shape[1]
    sds = jax.ShapeDtypeStruct
    return pl.pallas_call(
        _s5_prep_kernel,
        out_shape=[sds((g, 1, p), F32), sds((g, 1, p), F32), sds((g, i, p), F32), sds((g, i, p), F32)],
        name="s5_discretise",
    )(a_re.reshape(g, 1, p), a_im.reshape(g, 1, p), log_dt.reshape(g, 1, 1), bt_re, bt_im)


def _s5_scan_kernel(u_ref, bb_ref, cc_ref, d_ref, are_ref, aim_ref, s0_ref, y_ref, sf_ref, bu_ref, xb_ref, st_ref,
                    *, nb, tc, cin):
    t_step = pl.program_id(1)
    half = bu_ref.shape[2] // 2
    packed_rows = nb % BF16_ROWS == 0

    @pl.when(t_step == 0)
    def _():
        st_ref[...] = s0_ref[...]

    us = []
    for pk in range(2):
        u = u_ref[:, pk * cin:(pk + 1) * cin]
        us.append(u)
        bu_ref[pk] = jnp.dot(u, bb_ref[pk], preferred_element_type=F32)
    for pk in range(2):
        a_re = jnp.broadcast_to(are_ref[pk], (nb, half))
        a_im = jnp.broadcast_to(aim_ref[pk], (nb, half))
        x_re = st_ref[pk, :, 0:half]
        x_im = st_ref[pk, :, half:2 * half]
        for t in range(tc):
            rows = slice(t * nb, (t + 1) * nb)
            n_re = a_re * x_re - a_im * x_im + bu_ref[pk, rows, 0:half]
            n_im = a_re * x_im + a_im * x_re + bu_ref[pk, rows, half:2 * half]
            x_re, x_im = n_re, n_im
            if packed_rows:
                xb_ref[pk, rows, 0:half] = x_re.astype(BF16)
                xb_ref[pk, rows, half:2 * half] = x_im.astype(BF16)
            else:
                bu_ref[pk, rows, 0:half] = x_re
                bu_ref[pk, rows, half:2 * half] = x_im
        st_ref[pk, :, 0:half] = x_re
        st_ref[pk, :, half:2 * half] = x_im
    sf_ref[...] = st_ref[...]
    for pk in range(2):
        xs = xb_ref[pk] if packed_rows else bu_ref[pk].astype(BF16)
        y = jnp.dot(xs, cc_ref[pk], preferred_element_type=F32)
        y = y + d_ref[pk] * us[pk].astype(F32)
        y_ref[:, pk * cin:(pk + 1) * cin] = jax.nn.gelu(y).astype(BF16)


def _s5_scan(u_tm, bb, cc, dvec, abar_re, abar_im, s0, *, nb, l):
    d_ssm = u_tm.shape[1]
    npk = bb.shape[0]
    cin = bb.shape[1]
    two_half = bb.shape[2]
    tc = min(SSM_TIME, l)
    rows = tc * nb
    assert npk % 2 == 0
    kern = functools.partial(_s5_scan_kernel, nb=nb, tc=tc, cin=cin)
    pair = lambda p, t: (p, 0, 0)
    return pl.pallas_call(
        kern,
        grid=(npk // 2, l // tc),
        in_specs=[pl.BlockSpec((rows, 2 * cin), lambda p, t: (t, p)),
                  pl.BlockSpec((2, cin, two_half), pair),
                  pl.BlockSpec((2, two_half, cin), pair),
                  pl.BlockSpec((2, 1, cin), pair),
                  pl.BlockSpec((2, 1, two_half // 2), pair),
                  pl.BlockSpec((2, 1, two_half // 2), pair),
                  pl.BlockSpec((2, nb, two_half), pair)],
        out_specs=[pl.BlockSpec((rows, 2 * cin), lambda p, t: (t, p)),
                   pl.BlockSpec((2, nb, two_half), pair)],
        out_shape=[jax.ShapeDtypeStruct((l * nb, d_ssm), BF16),
                   jax.ShapeDtypeStruct((npk, nb, two_half), F32)],
        scratch_shapes=[pltpu.VMEM((2, rows, two_half), F32), pltpu.VMEM((2, rows, two_half), BF16),
                        pltpu.VMEM((2, nb, two_half), F32)],
        compiler_params=_cparams(("arbitrary", "arbitrary"), VMEM_LIMIT),
        name="s5_scan",
    )(u_tm, bb, cc, dvec, abar_re, abar_im, s0)


def _s5_pack_weights(bb_re, bb_im, c_re, c_im, d, abar_re, abar_im):
    g, i, p = bb_re.shape
    npk = g // SSM_PACK
    eye = jnp.eye(SSM_PACK, dtype=F32)

    def blockdiag_in(m):
        m = m.reshape(npk, SSM_PACK, i, p)
        return jnp.einsum("ngip,gh->ngihp", m, eye).reshape(npk, SSM_PACK * i, SSM_PACK * p)

    def blockdiag_out(m):
        m = m.reshape(npk, SSM_PACK, i, p)
        return jnp.einsum("ngip,gh->ngphi", m, eye).reshape(npk, SSM_PACK * p, SSM_PACK * i)

    bb = jnp.concatenate([blockdiag_in(bb_re), blockdiag_in(bb_im)], axis=2).astype(BF16)
    cc = jnp.concatenate([blockdiag_out(c_re), -blockdiag_out(c_im)], axis=1).astype(BF16)
    dvec = d.reshape(npk, 1, SSM_PACK * i)
    are = abar_re.reshape(npk, 1, SSM_PACK * p)
    aim = abar_im.reshape(npk, 1, SSM_PACK * p)
    return bb, cc, dvec, are, aim


def _state_to_packs(s_re, s_im):
    b, g, p = s_re.shape
    npk = g // SSM_PACK
    re = jnp.transpose(s_re.reshape(b, npk, SSM_PACK * p), (1, 0, 2))
    im = jnp.transpose(s_im.reshape(b, npk, SSM_PACK * p), (1, 0, 2))
    return jnp.concatenate([re, im], axis=2)


def _packs_to_state(sf, g, p):
    npk, b, two_half = sf.shape
    half = two_half // 2
    re = jnp.transpose(sf[:, :, :half], (1, 0, 2)).reshape(b, g, p)
    im = jnp.transpose(sf[:, :, half:], (1, 0, 2)).reshape(b, g, p)
    return re, im


def _outproj_kernel(attn_ref, ssm_ref, x_ref, mod_ref, wglu_ref, bglu_ref, wout_ref, gpm_ref, gpf_ref,
                    wr_ref, br_ref, x1_ref, h2_ref, lg_ref):
    y = ssm_ref[...]
    z = jnp.dot(y, wglu_ref[...], preferred_element_type=F32) + bglu_ref[...]
    s = (y.astype(F32) * jax.nn.sigmoid(z)).astype(BF16)
    m = jnp.dot(attn_ref[0], wout_ref[0:D_ATTN, :], preferred_element_type=F32)
    m = m + jnp.dot(s, wout_ref[D_ATTN:, :], preferred_element_type=F32)
    x1 = x_ref[0] + mod_ref[0, 2:3, :] * _rms(m, gpm_ref[...])
    x1_ref[0] = x1
    h2 = (_rms(x1, gpf_ref[...]) * (1.0 + mod_ref[0, 4:5, :]) + mod_ref[0, 3:4, :]).astype(BF16)
    h2_ref[...] = h2
    lg_ref[...] = jnp.dot(h2, wr_ref[...], preferred_element_type=F32) + br_ref[...]


def _outproj(attn, ssm2d, x, mod, w_glu_bf, b_glu, w_out_bf, g_post_mix, g_pre_ffn, w_router_bf, b_router):
    b, l, d = x.shape
    tm = min(ROW_TILE, l)
    nt = l // tm
    d_ssm = w_glu_bf.shape[0]
    ne = w_router_bf.shape[1]
    c2 = lambda bi, j: (0, 0)
    return pl.pallas_call(
        _outproj_kernel,
        grid=(b, nt),
        in_specs=[pl.BlockSpec((1, tm, D_ATTN), lambda bi, j: (bi, j, 0)),
                  pl.BlockSpec((tm, d_ssm), lambda bi, j: (j, bi)),
                  pl.BlockSpec((1, tm, d), lambda bi, j: (bi, j, 0)),
                  pl.BlockSpec((1, 6, d), lambda bi, j: (bi, 0, 0)),
                  pl.BlockSpec(w_glu_bf.shape, c2),
                  pl.BlockSpec((1, d_ssm), c2),
                  pl.BlockSpec(w_out_bf.shape, c2),
                  pl.BlockSpec((1, d), c2),
                  pl.BlockSpec((1, d), c2),
                  pl.BlockSpec(w_router_bf.shape, c2),
                  pl.BlockSpec((1, ne), c2)],
        out_specs=[pl.BlockSpec((1, tm, d), lambda bi, j: (bi, j, 0)),
                   pl.BlockSpec((tm, d), lambda bi, j: (bi * nt + j, 0)),
                   pl.BlockSpec((tm, ne), lambda bi, j: (bi * nt + j, 0))],
        out_shape=[jax.ShapeDtypeStruct((b, l, d), F32),
                   jax.ShapeDtypeStruct((b * l, d), BF16),
                   jax.ShapeDtypeStruct((b * l, ne), F32)],
        compiler_params=_cparams(("arbitrary", "arbitrary"), VMEM_LIMIT),
        name="glu_outproj_norms_router",
    )(attn, ssm2d, x, mod, w_glu_bf, b_glu.reshape(1, d_ssm), w_out_bf, g_post_mix.reshape(1, d),
      g_pre_ffn.reshape(1, d), w_router_bf, b_router.reshape(1, ne))


def _router_kernel(lg_ref, tri_ref, info_ref, pc_ref):
    lg = lg_ref[...]
    ne, tb = lg.shape
    eidx = lax.broadcasted_iota(I32, (ne, tb), 0).astype(F32)
    work = lg
    sels, vals = [], []
    for _ in range(TOP_K):
        m = work.max(axis=0, keepdims=True)
        idx = jnp.where(work == m, eidx, float(ne)).min(axis=0, keepdims=True)
        sel = eidx == idx
        sels.append(sel)
        vals.append(m)
        work = jnp.where(sel, -jnp.inf, work)
    ex = [jnp.exp(v - vals[0]) for v in vals]
    den = ex[0] + ex[1] + ex[2] + ex[3]
    onehot = (sels[0] | sels[1] | sels[2] | sels[3]).astype(F32)
    cnt = onehot.sum(axis=1, keepdims=True).astype(I32)
    pc = ((cnt + (BF16_ROWS - 1)) >> 4) << 4
    pc_ref[0] = pc
    rank = jnp.dot(onehot.astype(BF16), tri_ref[...], preferred_element_type=F32)
    low = jnp.where(lax.broadcasted_iota(I32, (ne, LANES), 1) < lax.broadcasted_iota(I32, (ne, LANES), 0), 1.0, 0.0)
    pcb = jnp.concatenate([jnp.broadcast_to(pc.astype(F32), (ne, LANES)), jnp.zeros((LANES - ne, LANES), F32)], axis=0)
    off = jnp.dot(low.astype(BF16), pcb.astype(BF16), preferred_element_type=F32)[:, 0:1]
    posmat = rank + off
    for k in range(TOP_K):
        info_ref[k:k + 1, :] = jnp.where(sels[k], posmat, 0.0).sum(axis=0, keepdims=True)
        info_ref[TOP_K + k:TOP_K + k + 1, :] = ex[k] / den


def _router(logits_t, tb):
    ne, t = logits_t.shape
    nb = t // tb
    tri = jnp.triu(jnp.ones((tb, tb), BF16), k=1)
    return pl.pallas_call(
        _router_kernel,
        grid=(nb,),
        in_specs=[pl.BlockSpec((ne, tb), lambda i: (0, i)),
                  pl.BlockSpec((tb, tb), lambda i: (0, 0))],
        out_specs=[pl.BlockSpec((2 * TOP_K, tb), lambda i: (0, i)),
                   pl.BlockSpec((1, ne, 1), lambda i: (i, 0, 0))],
        out_shape=[jax.ShapeDtypeStruct((2 * TOP_K, t), F32),
                   jax.ShapeDtypeStruct((nb, ne, 1), I32)],
        compiler_params=_cparams(("arbitrary",)),
        name="router_top4",
    )(logits_t, tri)


def _local_chunks(tb, ne):
    rows = TOP_K * tb + ne * (BF16_ROWS - 1)
    return -(-rows // SORT_ROWS) * (SORT_ROWS // BF16_ROWS)


def _sort_plan(pc, tb, n_tokens):
    nb, ne = pc.shape
    nch = _local_chunks(tb, ne)
    n_tiles = (TOP_K * n_tokens + nb * ne * (BF16_ROWS - 1)) // FFN_TILE + ne
    loc_end = jnp.cumsum(pc, axis=1)
    loc_off = loc_end - pc
    tot = pc.sum(axis=0)
    tot_pad = ((tot + FFN_TILE - 1) // FFN_TILE) * FFN_TILE
    base_end = jnp.cumsum(tot_pad)
    base = base_end - tot_pad
    glob = base[None, :] + jnp.cumsum(pc, axis=0) - pc
    n_used = (base_end[-1] // FFN_TILE).astype(I32)
    tile_row = jnp.arange(n_tiles, dtype=I32) * FFN_TILE
    tile_expert = jnp.minimum((base_end[None, :] <= tile_row[:, None]).sum(axis=1), ne - 1).astype(I32)
    r = jnp.arange(nch, dtype=I32) * BF16_ROWS
    e_of = jnp.minimum((loc_end[:, None, :] <= r[None, :, None]).sum(axis=2), ne - 1)
    shift = glob - loc_off
    owner = e_of[:, :, None] == jnp.arange(ne, dtype=I32)[None, None, :]
    dst = jnp.where(owner, shift[:, None, :], 0).sum(axis=2) + r[None, :]
    n_valid = (loc_end[:, -1] // BF16_ROWS).astype(I32)
    valid = jnp.arange(nch, dtype=I32)[None, :] < n_valid[:, None]
    spare = n_tiles * (FFN_TILE // BF16_ROWS) + (jnp.arange(nb, dtype=I32)[:, None] % 2) * nch \
        + jnp.arange(nch, dtype=I32)[None, :]
    chunk_dst = jnp.where(valid, dst // BF16_ROWS, spare).astype(I32)
    chunk_src = jnp.where(valid, dst // BF16_ROWS, 0).astype(I32)
    tail_n = ((tot_pad - tot) // BF16_ROWS).astype(I32)
    tail_dst = ((base + tot) // BF16_ROWS).astype(I32)
    spare_tiles = -(-2 * nch * BF16_ROWS // FFN_TILE)
    return dict(nch=nch, n_tiles=n_tiles, spare_tiles=spare_tiles, chunk_dst=chunk_dst.reshape(-1),
                chunk_src=chunk_src.reshape(-1), n_used=n_used, tile_expert=tile_expert, tail_n=tail_n,
                tail_dst=tail_dst)


def _chunk_copy(buf, hbm, slot, c, dst_chunk, sem, to_hbm):
    v = buf.at[slot, pl.ds(pl.multiple_of(c * BF16_ROWS, BF16_ROWS), BF16_ROWS)]
    h = hbm.at[pl.ds(pl.multiple_of(dst_chunk * BF16_ROWS, BF16_ROWS), BF16_ROWS)]
    return pltpu.make_async_copy(v, h, sem) if to_hbm else pltpu.make_async_copy(h, v, sem)


def _local_row_ids(rt, tb):
    return lax.broadcasted_iota(I32, (rt, tb), 0).astype(F32).astype(BF16)


def _local_pos(pos_row, r0, rt):
    return jnp.clip(pos_row - float(r0), -1.0, float(rt)).astype(BF16)


def _dispatch_kernel(cd_ref, tn_ref, td_ref, nu_ref, h_ref, info_ref, hx_ref, infox_ref, xs_ref,
                     buf_ref, zero_ref, ztile_ref, sem, zsem, *, nch, rt, n_tiles, spare_tiles):
    b = pl.program_id(0)
    nb = pl.num_programs(0)
    slot = b % 2

    def send(c0, c1):
        for c in range(c0, c1):
            _chunk_copy(buf_ref, xs_ref, slot, c, cd_ref[b * nch + c], sem.at[slot], True).start()

    def sort_block(hr, ir):
        tb = hr.shape[0]
        h = hr[...]
        rid = _local_row_ids(rt, tb)
        one = jnp.ones((rt, tb), BF16)
        nrows = _local_chunks(tb, N_EXPERTS) * BF16_ROWS
        for r0 in range(0, nrows, rt):
            p = jnp.zeros((rt, tb), BF16)
            for k in range(TOP_K):
                p = jnp.where(rid == _local_pos(ir[k:k + 1, :], r0, rt), one, p)
            buf_ref[slot, r0:r0 + rt, :] = jnp.dot(p, h, preferred_element_type=F32).astype(BF16)
            send(r0 // BF16_ROWS, (r0 + rt) // BF16_ROWS)
        send(nrows // BF16_ROWS, nch)

    pl.when(b < nb - 1)(lambda: sort_block(h_ref, info_ref))
    pl.when(b == nb - 1)(lambda: sort_block(hx_ref, infox_ref))

    def drain(step, sl):
        for c in range(nch):
            _chunk_copy(buf_ref, xs_ref, sl, c, cd_ref[step * nch + c], sem.at[sl], True).wait()

    @pl.when(b > 0)
    def _():
        drain(b - 1, 1 - slot)

    @pl.when(b == nb - 1)
    def _():
        drain(b, slot)
        zero_ref[...] = jnp.zeros_like(zero_ref)
        for e in range(N_EXPERTS):
            def zcopy(c, e=e):
                dst = pl.multiple_of((td_ref[e] + c) * BF16_ROWS, BF16_ROWS)
                return pltpu.make_async_copy(zero_ref, xs_ref.at[pl.ds(dst, BF16_ROWS)], zsem)

            lax.fori_loop(0, tn_ref[e], lambda c, _: (zcopy(c).start(), 0)[1], 0)
            lax.fori_loop(0, tn_ref[e], lambda c, _: (zcopy(c).wait(), 0)[1], 0)

        ztile_ref[...] = jnp.zeros_like(ztile_ref)

        def ztile(i):
            dst = pl.multiple_of(i * FFN_TILE, FFN_TILE)
            return pltpu.make_async_copy(ztile_ref, xs_ref.at[pl.ds(dst, FFN_TILE)], zsem)

        lax.fori_loop(nu_ref[0], n_tiles + spare_tiles, lambda i, _: (ztile(i).start(), 0)[1], 0)
        lax.fori_loop(nu_ref[0], n_tiles + spare_tiles, lambda i, _: (ztile(i).wait(), 0)[1], 0)


def _dispatch(h2, info, h2x, infox, plan, tb):
    t, d = h2.shape
    tx = h2x.shape[0]
    nb = t // tb
    nch = plan["nch"]
    rows = (plan["n_tiles"] + plan["spare_tiles"]) * FFN_TILE
    assert (nch * BF16_ROWS) % SORT_ROWS == 0 and tx <= tb
    kern = functools.partial(_dispatch_kernel, nch=nch, rt=SORT_ROWS, n_tiles=plan["n_tiles"],
                             spare_tiles=plan["spare_tiles"])
    grid_spec = pltpu.PrefetchScalarGridSpec(
        num_scalar_prefetch=4,
        grid=(nb + 1,),
        in_specs=[pl.BlockSpec((tb, d), lambda i, *_: (jnp.minimum(i, nb - 1), 0)),
                  pl.BlockSpec((2 * TOP_K, tb), lambda i, *_: (0, jnp.minimum(i, nb - 1))),
                  pl.BlockSpec((tx, d), lambda i, *_: (0, 0)),
                  pl.BlockSpec((2 * TOP_K, tx), lambda i, *_: (0, 0))],
        out_specs=pl.BlockSpec(memory_space=pl.ANY),
        scratch_shapes=[pltpu.VMEM((2, nch * BF16_ROWS, d), BF16),
                        pltpu.VMEM((BF16_ROWS, d), BF16),
                        pltpu.VMEM((FFN_TILE, d), BF16),
                        pltpu.SemaphoreType.DMA((2,)),
                        pltpu.SemaphoreType.DMA],
    )
    return pl.pallas_call(
        kern,
        grid_spec=grid_spec,
        out_shape=jax.ShapeDtypeStruct((rows, d), BF16),
        compiler_params=_cparams(("arbitrary",), VMEM_LIMIT),
        name="moe_dispatch",
    )(plan["chunk_dst"], plan["tail_n"], plan["tail_dst"], plan["n_used"].reshape(1), h2, info, h2x, infox)


def _ffn_kernel(te_ref, nu_ref, x_ref, wgu_ref, bgu_ref, wd_ref, bd_ref, perm_ref, y_ref, wgu_s, wd_s):
    i = pl.program_id(0)
    f = wd_s.shape[0]
    prev = te_ref[jnp.maximum(i - 1, 0)]
    new_expert = (i == 0) | (te_ref[i] != prev)

    @pl.when((i < nu_ref[0]) & new_expert)
    def _():
        for c in range(0, 2 * f, 2 * LANES):
            blk = wgu_ref[0, :, c:c + 2 * LANES].astype(BF16)
            wgu_s[:, c:c + 2 * LANES] = jnp.dot(blk, perm_ref[...], preferred_element_type=F32).astype(BF16)
        wd_s[...] = wd_ref[0].astype(BF16)

    @pl.when(i < nu_ref[0])
    def _():
        x = x_ref[...]
        acts = []
        for c in range(0, 2 * f, 2 * LANES):
            gu = jnp.dot(x, wgu_s[:, c:c + 2 * LANES], preferred_element_type=F32) + bgu_ref[0, :, c:c + 2 * LANES]
            glu = jnp.minimum(gu[:, :LANES], SWIGLU_LIMIT)
            lin = jnp.clip(gu[:, LANES:], -SWIGLU_LIMIT, SWIGLU_LIMIT)
            acts.append((glu * jax.nn.sigmoid(SWIGLU_ALPHA * glu) * (lin + 1.0)).astype(BF16))
        act = jnp.concatenate(acts, axis=1)
        y_ref[...] = (jnp.dot(act, wd_s[...], preferred_element_type=F32) + bd_ref[0]).astype(BF16)

    @pl.when(i >= nu_ref[0])
    def _():
        y_ref[...] = jnp.zeros_like(y_ref)


def _ffn(xs, plan, w_gate_up, b_gu_perm, w_down, b_down):
    d = xs.shape[1]
    ne, _, f2 = w_gate_up.shape
    f = f2 // 2
    n_tiles = plan["n_tiles"]
    rows = n_tiles * FFN_TILE
    src = jnp.concatenate([jnp.arange(LANES) * 2, jnp.arange(LANES) * 2 + 1])
    perm = (jnp.arange(2 * LANES)[:, None] == src[None, :]).astype(BF16)

    def xi(i, te, nu):
        return (jnp.minimum(i, nu[0] - 1), 0)

    def wi(i, te, nu):
        return (te[jnp.minimum(i, nu[0] - 1)], 0, 0)

    grid_spec = pltpu.PrefetchScalarGridSpec(
        num_scalar_prefetch=2,
        grid=(n_tiles,),
        in_specs=[pl.BlockSpec((FFN_TILE, d), xi),
                  pl.BlockSpec((1, d, f2), wi),
                  pl.BlockSpec((1, 1, f2), wi),
                  pl.BlockSpec((1, f, d), wi),
                  pl.BlockSpec((1, 1, d), wi),
                  pl.BlockSpec((2 * LANES, 2 * LANES), lambda i, te, nu: (0, 0))],
        out_specs=pl.BlockSpec((FFN_TILE, d), lambda i, te, nu: (i, 0)),
        scratch_shapes=[pltpu.VMEM((d, f2), BF16), pltpu.VMEM((f, d), BF16)],
    )
    return pl.pallas_call(
        _ffn_kernel,
        grid_spec=grid_spec,
        out_shape=jax.ShapeDtypeStruct((rows, d), BF16),
        compiler_params=_cparams(("arbitrary",), VMEM_LIMIT),
        name="moe_expert_ffn",
    )(plan["tile_expert"], plan["n_used"].reshape(1), xs, w_gate_up, b_gu_perm.reshape(ne, 1, f2), w_down,
      b_down.reshape(ne, 1, d), perm)


def _combine_kernel(cd_ref, ys_ref, info_ref, x1_ref, mod_ref, infox_ref, x1x_ref, modx_ref, g_ref,
                    o_ref, ox_ref, buf_ref, w_ref, sem, *, nch, rt):
    b = pl.program_id(0)
    nb = pl.num_programs(0)
    slot = b % 2

    def fetch(step, sl, c0, c1):
        for c in range(c0, c1):
            _chunk_copy(buf_ref, ys_ref, sl, c, cd_ref[step * nch + c], sem.at[sl], False).start()

    @pl.when(b == 0)
    def _():
        fetch(0, 0, 0, nch)

    def unsort_block(ir, x1r, modr, outr, prefetch_next):
        tb = x1r.shape[0]
        nbat = modr.shape[0]
        rb = tb // nbat
        nrows = _local_chunks(tb, N_EXPERTS) * BF16_ROWS
        per_tile = nch // (nrows // rt)
        rid = _local_row_ids(rt, tb)
        for j, r0 in enumerate(range(0, nrows, rt)):
            w = jnp.zeros((rt, tb), BF16)
            for k in range(TOP_K):
                gate = jnp.broadcast_to(ir[TOP_K + k:TOP_K + k + 1, :].astype(BF16), (rt, tb))
                w = jnp.where(rid == _local_pos(ir[k:k + 1, :], r0, rt), gate, w)
            w_ref[r0:r0 + rt, 0:tb] = w
            if prefetch_next:
                fetch(b + 1, 1 - slot, j * per_tile, (j + 1) * per_tile)

        for c in range(nch):
            _chunk_copy(buf_ref, ys_ref, slot, c, cd_ref[b * nch + c], sem.at[slot], False).wait()

        moe = lax.dot_general(w_ref[0:nrows, 0:tb], buf_ref[slot, 0:nrows, :], (((0,), (0,)), ((), ())),
                              preferred_element_type=F32)
        out = _rms(moe, g_ref[...])
        for s in range(nbat):
            rs = slice(s * rb, (s + 1) * rb)
            outr[rs, :] = x1r[rs, :] + modr[s, 5:6, :] * out[rs, :]

    pl.when(b < nb - 1)(lambda: unsort_block(info_ref, x1_ref, mod_ref, o_ref, True))
    pl.when(b == nb - 1)(lambda: unsort_block(infox_ref, x1x_ref, modx_ref, ox_ref, False))


def _combine(ys, info, x1, mod, infox, x1x, modx, g_post_ffn, plan, tb, l):
    t, d = x1.shape
    tx = x1x.shape[0]
    nb = t // tb
    nch = plan["nch"]
    assert l % tb == 0
    clamp = lambda i: jnp.minimum(i, nb - 1)
    kern = functools.partial(_combine_kernel, nch=nch, rt=SORT_ROWS)
    grid_spec = pltpu.PrefetchScalarGridSpec(
        num_scalar_prefetch=1,
        grid=(nb + 1,),
        in_specs=[pl.BlockSpec(memory_space=pl.ANY),
                  pl.BlockSpec((2 * TOP_K, tb), lambda i, *_: (0, clamp(i))),
                  pl.BlockSpec((tb, d), lambda i, *_: (clamp(i), 0)),
                  pl.BlockSpec((1, 6, d), lambda i, *_: (clamp(i) // (l // tb), 0, 0)),
                  pl.BlockSpec((2 * TOP_K, tx), lambda i, *_: (0, 0)),
                  pl.BlockSpec((tx, d), lambda i, *_: (0, 0)),
                  pl.BlockSpec(modx.shape, lambda i, *_: (0, 0, 0)),
                  pl.BlockSpec((1, d), lambda i, *_: (0, 0))],
        out_specs=[pl.BlockSpec((tb, d), lambda i, *_: (clamp(i), 0)),
                   pl.BlockSpec((tx, d), lambda i, *_: (0, 0))],
        scratch_shapes=[pltpu.VMEM((2, nch * BF16_ROWS, d), BF16), pltpu.VMEM((nch * BF16_ROWS, tb), BF16),
                        pltpu.SemaphoreType.DMA((2,))],
    )
    return pl.pallas_call(
        kern,
        grid_spec=grid_spec,
        out_shape=[jax.ShapeDtypeStruct((t, d), F32), jax.ShapeDtypeStruct((tx, d), F32)],
        compiler_params=_cparams(("arbitrary",), VMEM_LIMIT),
        name="moe_combine",
    )(plan["chunk_src"], ys, info, x1, mod, infox, x1x, modx, g_post_ffn.reshape(1, d))


def _moe(main, second, l, g_post_ffn, w_gate_up, b_gu_perm, w_down, b_down):
    h2, logits, x1, mod = main
    h2x, logitsx, x1x, modx = second
    t = h2.shape[0]
    tx = h2x.shape[0]
    tb = ROW_TILE
    info, pc = _router(logits[:, :N_EXPERTS].T, tb)
    infox, pcx = _router(logitsx[:, :N_EXPERTS].T, tx)
    pc_all = jnp.concatenate([pc.reshape(t // tb, N_EXPERTS), pcx.reshape(1, N_EXPERTS)], axis=0)
    plan = _sort_plan(pc_all, tb, t + tx)
    xs = _dispatch(h2, info, h2x, infox, plan, tb)
    ys = _ffn(xs, plan, w_gate_up, b_gu_perm, w_down, b_down)
    return _combine(ys, info, x1, mod, infox, x1x, modx, g_post_ffn, plan, tb, l)


def _stream(x, mod, wts, attn_fn, s0_packs):
    b, l, d = x.shape
    q, k_att, v_att, u2d, kf, vf = _inproj(x, mod, wts["g_pre_mix"], wts["w_in"], pad_front=attn_fn is None)
    if attn_fn is None:
        attn = _attn_prompt(q, k_att, v_att, wts["table_prompt"])
    else:
        attn = attn_fn(q, k_att, v_att)
    d_ssm = u2d.shape[1] // b
    u_tm = u2d.reshape(l * b, d_ssm)
    y_tm, sf = _s5_scan(u_tm, wts["bb"], wts["cc"], wts["dvec"], wts["are"], wts["aim"], s0_packs, nb=b, l=l)
    ssm2d = y_tm.reshape(l, b * d_ssm)
    x1, h2, logits = _outproj(attn, ssm2d, x, mod, wts["w_glu"], wts["b_glu"], wts["w_out"], wts["g_post_mix"],
                              wts["g_pre_ffn"], wts["w_router"], wts["b_router"])
    return (h2, logits, x1.reshape(b * l, d), mod), kf, vf, sf


def kernel(x_prompt, x_sample, cache_k, cache_v, state_ssm_re, state_ssm_im, c_prompt, c_sample, w_ada, b_ada, g_pre_mix, g_post_mix, g_pre_ffn, g_post_ffn, w_in, rel_bias, ssm_a_re, ssm_a_im, ssm_log_dt, ssm_b_re, ssm_b_im, ssm_c_re, ssm_c_im, ssm_d, w_glu, b_glu, w_out, w_router, b_router, w_gate_up, b_gate_up, w_down, b_down):
    depth = w_ada.shape[0]
    assert depth == 1, "single-layer trunk"
    bp, lp, d = x_prompt.shape
    bs, ls, _ = x_sample.shape
    n_cache = cache_k.shape[2]
    past_len = PAST_LEN
    g, p = ssm_a_re.shape[1:]
    lyr = 0

    mod = _ada(jnp.concatenate([c_prompt, c_sample], axis=0), w_ada[lyr], b_ada[lyr])
    mod = mod.reshape(bp + bs, 6, d)

    abre, abim, bbre, bbim = _s5_prep(ssm_a_re[lyr], ssm_a_im[lyr], ssm_log_dt[lyr], ssm_b_re[lyr], ssm_b_im[lyr])
    bb, cc, dvec, are, aim = _s5_pack_weights(bbre, bbim, ssm_c_re[lyr], ssm_c_im[lyr], ssm_d[lyr], abre, abim)

    f2 = w_gate_up.shape[-1]
    b_gu_perm = b_gate_up[lyr].reshape(N_EXPERTS, f2 // (2 * LANES), LANES, 2)
    b_gu_perm = jnp.transpose(b_gu_perm, (0, 1, 3, 2)).reshape(N_EXPERTS, f2)

    rb = rel_bias[lyr]
    wts = dict(
        g_pre_mix=g_pre_mix[lyr], g_post_mix=g_post_mix[lyr], g_pre_ffn=g_pre_ffn[lyr], g_post_ffn=g_post_ffn[lyr],
        w_in=w_in[lyr].astype(BF16), w_glu=w_glu[lyr].astype(BF16), b_glu=b_glu[lyr], w_out=w_out[lyr].astype(BF16),
        w_router=jnp.pad(w_router[lyr], ((0, 0), (0, LANES - N_EXPERTS))).astype(BF16),
        b_router=jnp.pad(b_router[lyr], (0, LANES - N_EXPERTS)),
        w_gate_up=w_gate_up[lyr], b_gu_perm=b_gu_perm, w_down=w_down[lyr], b_down=b_down[lyr],
        bb=bb, cc=cc, dvec=dvec, are=are, aim=aim,
        table_prompt=_bias_table(rb, q0=WIN_ROWS, k0=0, nq=CHUNK, nk=Q_TILE + WIN_ROWS, strips=Q_TILE // CHUNK),
    )

    zero_state = jnp.zeros((g // SSM_PACK, bp, 2 * SSM_PACK * p), F32)
    ffn_in_p, kf_p, vf_p, sf_p = _stream(x_prompt, mod[:bp], wts, None, zero_state)

    tabc = _bias_table(rb, q0=past_len, k0=past_len - n_cache, nq=ls, nk=n_cache)
    tabn = _bias_table(rb, q0=past_len, k0=past_len, nq=ls, nk=ls)
    ck = cache_k[lyr].reshape(bs, n_cache, D_ATTN)
    cv = cache_v[lyr].reshape(bs, n_cache, D_ATTN)
    attn_s = lambda q, k, v: _attn_sample(q, k, v, ck, cv, tabc, tabn)
    s0 = _state_to_packs(state_ssm_re[lyr], state_ssm_im[lyr])
    ffn_in_s, kf_s, vf_s, sf_s = _stream(x_sample, mod[bp:], wts, attn_s, s0)

    y_p, y_s = _moe(ffn_in_p, ffn_in_s, lp, wts["g_post_ffn"], wts["w_gate_up"], wts["b_gu_perm"], wts["w_down"],
                    wts["b_down"])
    y_p = y_p.reshape(bp, lp, d)
    y_s = y_s.reshape(bs, ls, d)

    def heads(a):
        return a.reshape(1, a.shape[0], a.shape[1], N_HEADS, HEAD_DIM)

    sre_p, sim_p = _packs_to_state(sf_p, g, p)
    sre_s, sim_s = _packs_to_state(sf_s, g, p)
    return (y_p, y_s, heads(kf_p), heads(vf_p), sre_p[None], sim_p[None],
            heads(kf_s), heads(vf_s), sre_s[None], sim_s[None])
```

```python
import functools
import math

import jax
import jax.numpy as jnp
from jax import lax
from jax.experimental import pallas as pl
from jax.experimental.pallas import tpu as pltpu

F32 = jnp.float32
BF16 = jnp.bfloat16
I32 = jnp.int32
HIGHEST = lax.Precision.HIGHEST

CHUNK = 64
CHUNK_LOG2 = 6
N_PAST_CHUNKS = 8
PAST_LEN = 2048
WIN_ROWS = N_PAST_CHUNKS * CHUNK
MAX_REL = 64
N_REL = 2 * MAX_REL + 1
N_HEADS = 8
HEAD_DIM = 64
D_ATTN = N_HEADS * HEAD_DIM
SSM_GROUP = 16
SSM_STATE = 64
N_EXPERTS = 32
TOP_K = 4
SWIGLU_LIMIT = 7.0
SWIGLU_ALPHA = 1.702
RMS_EPS = 1e-6
NEG_INF = -1e30
LOG2E = math.log2(math.e)

LANES = 128
BF16_ROWS = 16
VMEM_LIMIT = 56 * 1024 * 1024

ROW_TILE = 512
Q_TILE = 256
ATTN_QSUB = 128
ATTN_KBLK = 256
SSM_PACK = 8
SSM_TIME = 64
SSM_TSUB = 16
FFN_TILE = 512
SORT_ROWS = 256


def _cparams(sem, vmem=None):
    return pltpu.CompilerParams(dimension_semantics=sem, vmem_limit_bytes=vmem)


def _rms(x, g):
    return x * lax.rsqrt(jnp.mean(x * x, axis=-1, keepdims=True) + RMS_EPS) * g


def _ada_kernel(c_ref, w_ref, b_ref, o_ref):
    c = c_ref[...]
    s = c * jax.nn.sigmoid(c)
    o_ref[...] = jnp.dot(s, w_ref[...], precision=HIGHEST, preferred_element_type=F32) + b_ref[...]


def _ada(c, w_ada, b_ada):
    n, d = c.shape
    nout = w_ada.shape[1]
    return pl.pallas_call(
        _ada_kernel,
        grid=(nout // d,),
        in_specs=[pl.BlockSpec((n, d), lambda j: (0, 0)),
                  pl.BlockSpec((d, d), lambda j: (0, j)),
                  pl.BlockSpec((1, d), lambda j: (0, j))],
        out_specs=pl.BlockSpec((n, d), lambda j: (0, j)),
        out_shape=jax.ShapeDtypeStruct((n, nout), F32),
        compiler_params=_cparams(("arbitrary",), VMEM_LIMIT),
        name="ada_mod",
    )(c, w_ada, b_ada.reshape(1, nout))


def _inproj_kernel(x_ref, mod_ref, g_ref, w_ref, q_ref, k_ref, v_ref, u_ref, kf_ref, vf_ref, *, first_keep, pad):
    j = pl.program_id(1)

    def compute():
        x = x_ref[0]
        h = _rms(x, g_ref[...]) * (1.0 + mod_ref[0, 1:2, :]) + mod_ref[0, 0:1, :]
        proj = jnp.dot(h.astype(BF16), w_ref[...], preferred_element_type=F32)
        q_ref[0] = (proj[:, :D_ATTN] * (HEAD_DIM ** -0.5 * LOG2E)).astype(BF16)
        k_ref[0] = proj[:, D_ATTN:2 * D_ATTN].astype(BF16)
        v_ref[0] = proj[:, 2 * D_ATTN:3 * D_ATTN].astype(BF16)
        u_ref[0] = proj[:, 3 * D_ATTN:].astype(BF16)

        @pl.when(j >= first_keep + pad)
        def _():
            kf_ref[0] = proj[:, D_ATTN:2 * D_ATTN]
            vf_ref[0] = proj[:, 2 * D_ATTN:3 * D_ATTN]

    if pad:
        @pl.when(j == 0)
        def _():
            k_ref[...] = jnp.zeros_like(k_ref)
            v_ref[...] = jnp.zeros_like(v_ref)

        pl.when(j > 0)(compute)
    else:
        compute()


def _inproj(x, mod, g, w_in_bf, *, pad_front):
    b, l, d = x.shape
    tm = min(ROW_TILE, l)
    nt = l // tm
    keep = min(WIN_ROWS, l)
    first_keep = (l - keep) // tm
    pad = 1 if pad_front else 0
    d_ssm = w_in_bf.shape[1] - 3 * D_ATTN
    assert l % tm == 0 and keep % tm == 0 and (not pad_front or tm == WIN_ROWS)

    def xi(bi, j):
        return (bi, jnp.maximum(j - pad, 0), 0)

    def kfi(bi, j):
        return (bi, jnp.maximum(j - pad - first_keep, 0), 0)

    kern = functools.partial(_inproj_kernel, first_keep=first_keep, pad=pad)
    return pl.pallas_call(
        kern,
        grid=(b, nt + pad),
        in_specs=[pl.BlockSpec((1, tm, d), xi),
                  pl.BlockSpec((1, 6, d), lambda bi, j: (bi, 0, 0)),
                  pl.BlockSpec((1, d), lambda bi, j: (0, 0)),
                  pl.BlockSpec(w_in_bf.shape, lambda bi, j: (0, 0))],
        out_specs=[pl.BlockSpec((1, tm, D_ATTN), xi),
                   pl.BlockSpec((1, tm, D_ATTN), lambda bi, j: (bi, j, 0)),
                   pl.BlockSpec((1, tm, D_ATTN), lambda bi, j: (bi, j, 0)),
                   pl.BlockSpec((1, tm, d_ssm), xi),
                   pl.BlockSpec((1, tm, D_ATTN), kfi),
                   pl.BlockSpec((1, tm, D_ATTN), kfi)],
        out_shape=[jax.ShapeDtypeStruct((b, l, D_ATTN), BF16),
                   jax.ShapeDtypeStruct((b, l + pad * tm, D_ATTN), BF16),
                   jax.ShapeDtypeStruct((b, l + pad * tm, D_ATTN), BF16),
                   jax.ShapeDtypeStruct((b, l, d_ssm), BF16),
                   jax.ShapeDtypeStruct((b, keep, D_ATTN), F32),
                   jax.ShapeDtypeStruct((b, keep, D_ATTN), F32)],
        compiler_params=_cparams(("arbitrary", "arbitrary"), VMEM_LIMIT),
        name="prenorm_inproj",
    )(x, mod, g.reshape(1, d), w_in_bf)


def _bias_kernel(rb_ref, o_ref, *, q0, k0, nq, nk, strips):
    h = pl.program_id(0)
    qpos = q0 + lax.broadcasted_iota(I32, (nq, nk), 0)
    kpos = k0 + lax.broadcasted_iota(I32, (nq, nk), 1)
    rel = jnp.clip(kpos - qpos, -MAX_REL, MAX_REL) + MAX_REL
    qc = qpos >> CHUNK_LOG2
    kc = kpos >> CHUNK_LOG2
    allowed = (kc <= qc) & (kc >= qc - N_PAST_CHUNKS)

    def body(r, acc):
        return jnp.where(rel == r, rb_ref[r, h], acc)

    val = lax.fori_loop(0, N_REL, body, jnp.zeros((nq, nk), F32))
    strip = jnp.where(allowed, val * LOG2E, NEG_INF)
    o_ref[0, 0:nq, :] = strip
    for c in range(1, strips):
        o_ref[0, c * nq:(c + 1) * nq, :] = pltpu.roll(strip, c * CHUNK, axis=1)


def _bias_table(rel_bias, *, q0, k0, nq, nk, strips=1):
    kern = functools.partial(_bias_kernel, q0=q0, k0=k0, nq=nq, nk=nk, strips=strips)
    return pl.pallas_call(
        kern,
        grid=(N_HEADS,),
        in_specs=[pl.BlockSpec(memory_space=pltpu.SMEM)],
        out_specs=pl.BlockSpec((1, strips * nq, nk), lambda h: (h, 0, 0)),
        out_shape=jax.ShapeDtypeStruct((N_HEADS, strips * nq, nk), F32),
        compiler_params=_cparams(("arbitrary",)),
        name="bias_table",
    )(rel_bias)


def _softmax_pv(scores, values):
    m = scores[0].max(axis=-1, keepdims=True)
    for s in scores[1:]:
        m = jnp.maximum(m, s.max(axis=-1, keepdims=True))
    acc = None
    den = None
    for s, v in zip(scores, values):
        p = jnp.exp2(s - m)
        d = p.sum(axis=-1, keepdims=True)
        o = jnp.dot(p.astype(BF16), v, preferred_element_type=F32)
        acc = o if acc is None else acc + o
        den = d if den is None else den + d
    return acc / den


def _attn_prompt_kernel(q_ref, k_ref, v_ref, tab_ref, o_ref, *, tq, band):
    qi = pl.program_id(1)
    t0 = pl.multiple_of(qi * tq, tq)
    lane = lax.broadcasted_iota(I32, (1, LANES), 1)

    def body(mask_front):
        if mask_front:
            col = lax.broadcasted_iota(I32, (1, band), 1)
            colmask = jnp.where(col + t0 >= WIN_ROWS, 0.0, NEG_INF).astype(F32)
        for hp in range(N_HEADS // 2):
            ls = slice(hp * LANES, (hp + 1) * LANES)
            for q0 in range(0, tq, ATTN_QSUB):
                rows = slice(q0, q0 + ATTN_QSUB)
                q2 = q_ref[0, rows, ls]
                outs = []
                for a in range(2):
                    qa = jnp.where((lane < HEAD_DIM) == (a == 0), q2, jnp.zeros_like(q2))
                    m = den = acc = None
                    for k0 in range(0, band, ATTN_KBLK):
                        cols = slice(k0, k0 + ATTN_KBLK)
                        kk = k_ref[0, pl.ds(t0 + k0, ATTN_KBLK), ls]
                        vv = v_ref[0, pl.ds(t0 + k0, ATTN_KBLK), ls]
                        s = lax.dot_general(qa, kk, (((1,), (1,)), ((), ())), preferred_element_type=F32)
                        s = s + tab_ref[2 * hp + a, rows, cols]
                        if mask_front:
                            s = s + colmask[:, cols]
                        mb = s.max(axis=-1, keepdims=True)
                        if m is None:
                            m = mb
                            p = jnp.exp2(s - m)
                            den = p.sum(axis=-1, keepdims=True)
                            acc = jnp.dot(p.astype(BF16), vv, preferred_element_type=F32)
                        else:
                            m_new = jnp.maximum(m, mb)
                            alpha = jnp.exp2(m - m_new)
                            p = jnp.exp2(s - m_new)
                            den = alpha * den + p.sum(axis=-1, keepdims=True)
                            acc = alpha * acc + jnp.dot(p.astype(BF16), vv, preferred_element_type=F32)
                            m = m_new
                    outs.append(acc / den)
                o_ref[0, rows, ls] = jnp.where(lane < HEAD_DIM, outs[0], outs[1]).astype(BF16)

    pl.when(t0 < WIN_ROWS)(lambda: body(True))
    pl.when(t0 >= WIN_ROWS)(lambda: body(False))


def _attn_prompt(q, k_pad, v_pad, table):
    b, l, _ = q.shape
    tq = Q_TILE
    band = tq + WIN_ROWS
    kern = functools.partial(_attn_prompt_kernel, tq=tq, band=band)
    return pl.pallas_call(
        kern,
        grid=(b, l // tq),
        in_specs=[pl.BlockSpec((1, tq, D_ATTN), lambda bi, i: (bi, i, 0)),
                  pl.BlockSpec((1, l + WIN_ROWS, D_ATTN), lambda bi, i: (bi, 0, 0)),
                  pl.BlockSpec((1, l + WIN_ROWS, D_ATTN), lambda bi, i: (bi, 0, 0)),
                  pl.BlockSpec(table.shape, lambda bi, i: (0, 0, 0))],
        out_specs=pl.BlockSpec((1, tq, D_ATTN), lambda bi, i: (bi, i, 0)),
        out_shape=jax.ShapeDtypeStruct((b, l, D_ATTN), BF16),
        compiler_params=_cparams(("arbitrary", "arbitrary"), VMEM_LIMIT),
        name="attn_prompt",
    )(q, k_pad, v_pad, table)


def _attn_sample_kernel(q_ref, k_ref, v_ref, ck_ref, cv_ref, tabc_ref, tabn_ref, o_ref):
    lane = lax.broadcasted_iota(I32, (1, LANES), 1)
    for hp in range(N_HEADS // 2):
        ls = slice(hp * LANES, (hp + 1) * LANES)
        q2 = q_ref[0, :, ls]
        kn = k_ref[0, :, ls]
        vn = v_ref[0, :, ls]
        kc = ck_ref[0, :, ls].astype(BF16)
        vc = cv_ref[0, :, ls].astype(BF16)
        outs = []
        for a in range(2):
            h = 2 * hp + a
            qa = jnp.where((lane < HEAD_DIM) == (a == 0), q2, jnp.zeros_like(q2))
            sc = lax.dot_general(qa, kc, (((1,), (1,)), ((), ())), preferred_element_type=F32) + tabc_ref[h]
            sn = lax.dot_general(qa, kn, (((1,), (1,)), ((), ())), preferred_element_type=F32) + tabn_ref[h]
            outs.append(_softmax_pv([sc, sn], [vc, vn]))
        o_ref[0, :, ls] = jnp.where(lane < HEAD_DIM, outs[0], outs[1]).astype(BF16)


def _attn_sample(q, k, v, cache_k, cache_v, tabc, tabn):
    b, s, _ = q.shape
    nc = cache_k.shape[1]
    return pl.pallas_call(
        _attn_sample_kernel,
        grid=(b,),
        in_specs=[pl.BlockSpec((1, s, D_ATTN), lambda bi: (bi, 0, 0)),
                  pl.BlockSpec((1, s, D_ATTN), lambda bi: (bi, 0, 0)),
                  pl.BlockSpec((1, s, D_ATTN), lambda bi: (bi, 0, 0)),
                  pl.BlockSpec((1, nc, D_ATTN), lambda bi: (bi, 0, 0)),
                  pl.BlockSpec((1, nc, D_ATTN), lambda bi: (bi, 0, 0)),
                  pl.BlockSpec(tabc.shape, lambda bi: (0, 0, 0)),
                  pl.BlockSpec(tabn.shape, lambda bi: (0, 0, 0))],
        out_specs=pl.BlockSpec((1, s, D_ATTN), lambda bi: (bi, 0, 0)),
        out_shape=jax.ShapeDtypeStruct((b, s, D_ATTN), BF16),
        compiler_params=_cparams(("arbitrary",), VMEM_LIMIT),
        name="attn_sample",
    )(q, k, v, cache_k, cache_v, tabc, tabn)


def _s5_prep_kernel(are_ref, aim_ref, ldt_ref, bre_ref, bim_ref, abre_ref, abim_ref, bbre_ref, bbim_ref):
    a_re = are_ref[...]
    a_im = aim_ref[...]
    dt = jnp.exp(ldt_ref[...])
    mag = jnp.exp(dt * a_re)
    abar_re = mag * jnp.cos(dt * a_im)
    abar_im = mag * jnp.sin(dt * a_im)
    den = a_re * a_re + a_im * a_im
    num_re = abar_re - 1.0
    coef_re = (num_re * a_re + abar_im * a_im) / den
    coef_im = (abar_im * a_re - num_re * a_im) / den
    abre_ref[...] = abar_re
    abim_ref[...] = abar_im
    b_re = bre_ref[...]
    b_im = bim_ref[...]
    bbre_ref[...] = coef_re * b_re - coef_im * b_im
    bbim_ref[...] = coef_re * b_im + coef_im * b_re


def _s5_prep(a_re, a_im, log_dt, b_re, b_im):
    g, p = a_re.shape
    bt_re = jnp.transpose(b_re, (0, 2, 1))
    bt_im = jnp.transpose(b_im, (0, 2, 1))
    i = bt_re.shape[1]
    sds = jax.ShapeDtypeStruct
    return pl.pallas_call(
        _s5_prep_kernel,
        out_shape=[sds((g, 1, p), F32), sds((g, 1, p), F32), sds((g, i, p), F32), sds((g, i, p), F32)],
        name="s5_discretise",
    )(a_re.reshape(g, 1, p), a_im.reshape(g, 1, p), log_dt.reshape(g, 1, 1), bt_re, bt_im)


def _s5_scan_kernel(u_ref, perm_ref, permt_ref, bb_ref, cc_ref, d_ref, are_ref, aim_ref, s0_ref, y_ref, sf_ref,
                    utb_ref, ytb_ref, bu_ref, xb_ref, st_ref, *, nb, tc, cin, tsub):
    t_step = pl.program_id(1)
    half = bu_ref.shape[2] // 2
    packed_rows = nb % BF16_ROWS == 0
    sub = nb * tsub

    @pl.when(t_step == 0)
    def _():
        st_ref[...] = s0_ref[...]

    for s in range(tc // tsub):
        blk = u_ref[:, s * tsub:(s + 1) * tsub, :].reshape(sub, 2 * cin)
        utb_ref[s * sub:(s + 1) * sub, :] = jnp.dot(perm_ref[...], blk, preferred_element_type=F32).astype(BF16)

    us = []
    for pk in range(2):
        u = utb_ref[:, pk * cin:(pk + 1) * cin]
        us.append(u)
        bu_ref[pk] = jnp.dot(u, bb_ref[pk], preferred_element_type=F32)
    for pk in range(2):
        a_re = jnp.broadcast_to(are_ref[pk], (nb, half))
        a_im = jnp.broadcast_to(aim_ref[pk], (nb, half))
        x_re = st_ref[pk, :, 0:half]
        x_im = st_ref[pk, :, half:2 * half]
        for t in range(tc):
            rows = slice(t * nb, (t + 1) * nb)
            n_re = a_re * x_re - a_im * x_im + bu_ref[pk, rows, 0:half]
            n_im = a_re * x_im + a_im * x_re + bu_ref[pk, rows, half:2 * half]
            x_re, x_im = n_re, n_im
            if packed_rows:
                xb_ref[pk, rows, 0:half] = x_re.astype(BF16)
                xb_ref[pk, rows, half:2 * half] = x_im.astype(BF16)
            else:
                bu_ref[pk, rows, 0:half] = x_re
                bu_ref[pk, rows, half:2 * half] = x_im
        st_ref[pk, :, 0:half] = x_re
        st_ref[pk, :, half:2 * half] = x_im
    sf_ref[...] = st_ref[...]
    for pk in range(2):
        xs = xb_ref[pk] if packed_rows else bu_ref[pk].astype(BF16)
        y = jnp.dot(xs, cc_ref[pk], preferred_element_type=F32)
        y = y + d_ref[pk] * us[pk].astype(F32)
        ytb_ref[:, pk * cin:(pk + 1) * cin] = jax.nn.gelu(y).astype(BF16)
    for s in range(tc // tsub):
        blk = jnp.dot(permt_ref[...], ytb_ref[s * sub:(s + 1) * sub, :], preferred_element_type=F32)
        y_ref[:, s * tsub:(s + 1) * tsub, :] = blk.astype(BF16).reshape(nb, tsub, 2 * cin)


def _s5_scan(u, bb, cc, dvec, abar_re, abar_im, s0):
    nb, l, d_ssm = u.shape
    npk = bb.shape[0]
    cin = bb.shape[1]
    two_half = bb.shape[2]
    tc = min(SSM_TIME, l)
    tsub = min(SSM_TSUB, tc)
    rows = tc * nb
    assert npk % 2 == 0 and tc % tsub == 0
    src = (jnp.arange(nb * tsub) % nb) * tsub + jnp.arange(nb * tsub) // nb
    perm = (src[:, None] == jnp.arange(nb * tsub)[None, :]).astype(BF16)
    kern = functools.partial(_s5_scan_kernel, nb=nb, tc=tc, cin=cin, tsub=tsub)
    pair = lambda p, t: (p, 0, 0)
    return pl.pallas_call(
        kern,
        grid=(npk // 2, l // tc),
        in_specs=[pl.BlockSpec((nb, tc, 2 * cin), lambda p, t: (0, t, p)),
                  pl.BlockSpec(perm.shape, lambda p, t: (0, 0)),
                  pl.BlockSpec(perm.shape, lambda p, t: (0, 0)),
                  pl.BlockSpec((2, cin, two_half), pair),
                  pl.BlockSpec((2, two_half, cin), pair),
                  pl.BlockSpec((2, 1, cin), pair),
                  pl.BlockSpec((2, 1, two_half // 2), pair),
                  pl.BlockSpec((2, 1, two_half // 2), pair),
                  pl.BlockSpec((2, nb, two_half), pair)],
        out_specs=[pl.BlockSpec((nb, tc, 2 * cin), lambda p, t: (0, t, p)),
                   pl.BlockSpec((2, nb, two_half), pair)],
        out_shape=[jax.ShapeDtypeStruct((nb, l, d_ssm), BF16),
                   jax.ShapeDtypeStruct((npk, nb, two_half), F32)],
        scratch_shapes=[pltpu.VMEM((rows, 2 * cin), BF16), pltpu.VMEM((rows, 2 * cin), BF16),
                        pltpu.VMEM((2, rows, two_half), F32), pltpu.VMEM((2, rows, two_half), BF16),
                        pltpu.VMEM((2, nb, two_half), F32)],
        compiler_params=_cparams(("arbitrary", "arbitrary"), VMEM_LIMIT),
        name="s5_scan",
    )(u, perm, perm.T, bb, cc, dvec, abar_re, abar_im, s0)


def _s5_pack_weights(bb_re, bb_im, c_re, c_im, d, abar_re, abar_im):
    g, i, p = bb_re.shape
    npk = g // SSM_PACK
    eye = jnp.eye(SSM_PACK, dtype=F32)

    def blockdiag_in(m):
        m = m.reshape(npk, SSM_PACK, i, p)
        return jnp.einsum("ngip,gh->ngihp", m, eye).reshape(npk, SSM_PACK * i, SSM_PACK * p)

    def blockdiag_out(m):
        m = m.reshape(npk, SSM_PACK, i, p)
        return jnp.einsum("ngip,gh->ngphi", m, eye).reshape(npk, SSM_PACK * p, SSM_PACK * i)

    bb = jnp.concatenate([blockdiag_in(bb_re), blockdiag_in(bb_im)], axis=2).astype(BF16)
    cc = jnp.concatenate([blockdiag_out(c_re), -blockdiag_out(c_im)], axis=1).astype(BF16)
    dvec = d.reshape(npk, 1, SSM_PACK * i)
    are = abar_re.reshape(npk, 1, SSM_PACK * p)
    aim = abar_im.reshape(npk, 1, SSM_PACK * p)
    return bb, cc, dvec, are, aim


def _state_to_packs(s_re, s_im):
    b, g, p = s_re.shape
    npk = g // SSM_PACK
    re = jnp.transpose(s_re.reshape(b, npk, SSM_PACK * p), (1, 0, 2))
    im = jnp.transpose(s_im.reshape(b, npk, SSM_PACK * p), (1, 0, 2))
    return jnp.concatenate([re, im], axis=2)


def _packs_to_state(sf, g, p):
    npk, b, two_half = sf.shape
    half = two_half // 2
    re = jnp.transpose(sf[:, :, :half], (1, 0, 2)).reshape(b, g, p)
    im = jnp.transpose(sf[:, :, half:], (1, 0, 2)).reshape(b, g, p)
    return re, im


def _outproj_kernel(attn_ref, ssm_ref, x_ref, mod_ref, wglu_ref, bglu_ref, wout_ref, gpm_ref, gpf_ref,
                    wr_ref, br_ref, x1_ref, h2_ref, lg_ref):
    y = ssm_ref[0]
    z = jnp.dot(y, wglu_ref[...], preferred_element_type=F32) + bglu_ref[...]
    s = (y.astype(F32) * jax.nn.sigmoid(z)).astype(BF16)
    m = jnp.dot(attn_ref[0], wout_ref[0:D_ATTN, :], preferred_element_type=F32)
    m = m + jnp.dot(s, wout_ref[D_ATTN:, :], preferred_element_type=F32)
    x1 = x_ref[0] + mod_ref[0, 2:3, :] * _rms(m, gpm_ref[...])
    x1_ref[0] = x1
    h2 = (_rms(x1, gpf_ref[...]) * (1.0 + mod_ref[0, 4:5, :]) + mod_ref[0, 3:4, :]).astype(BF16)
    h2_ref[...] = h2
    lg_ref[...] = jnp.dot(h2, wr_ref[...], preferred_element_type=F32) + br_ref[...]


def _outproj(attn, ssm2d, x, mod, w_glu_bf, b_glu, w_out_bf, g_post_mix, g_pre_ffn, w_router_bf, b_router):
    b, l, d = x.shape
    tm = min(ROW_TILE, l)
    nt = l // tm
    d_ssm = w_glu_bf.shape[0]
    ne = w_router_bf.shape[1]
    c2 = lambda bi, j: (0, 0)
    return pl.pallas_call(
        _outproj_kernel,
        grid=(b, nt),
        in_specs=[pl.BlockSpec((1, tm, D_ATTN), lambda bi, j: (bi, j, 0)),
                  pl.BlockSpec((1, tm, d_ssm), lambda bi, j: (bi, j, 0)),
                  pl.BlockSpec((1, tm, d), lambda bi, j: (bi, j, 0)),
                  pl.BlockSpec((1, 6, d), lambda bi, j: (bi, 0, 0)),
                  pl.BlockSpec(w_glu_bf.shape, c2),
                  pl.BlockSpec((1, d_ssm), c2),
                  pl.BlockSpec(w_out_bf.shape, c2),
                  pl.BlockSpec((1, d), c2),
                  pl.BlockSpec((1, d), c2),
                  pl.BlockSpec(w_router_bf.shape, c2),
                  pl.BlockSpec((1, ne), c2)],
        out_specs=[pl.BlockSpec((1, tm, d), lambda bi, j: (bi, j, 0)),
                   pl.BlockSpec((tm, d), lambda bi, j: (bi * nt + j, 0)),
                   pl.BlockSpec((tm, ne), lambda bi, j: (bi * nt + j, 0))],
        out_shape=[jax.ShapeDtypeStruct((b, l, d), F32),
                   jax.ShapeDtypeStruct((b * l, d), BF16),
                   jax.ShapeDtypeStruct((b * l, ne), F32)],
        compiler_params=_cparams(("arbitrary", "arbitrary"), VMEM_LIMIT),
        name="glu_outproj_norms_router",
    )(attn, ssm2d, x, mod, w_glu_bf, b_glu.reshape(1, d_ssm), w_out_bf, g_post_mix.reshape(1, d),
      g_pre_ffn.reshape(1, d), w_router_bf, b_router.reshape(1, ne))


def _router_kernel(lg_ref, tri_ref, info_ref, pc_ref):
    lg = lg_ref[...]
    ne, tb = lg.shape
    eidx = lax.broadcasted_iota(I32, (ne, tb), 0).astype(F32)
    work = lg
    sels, vals = [], []
    for _ in range(TOP_K):
        m = work.max(axis=0, keepdims=True)
        idx = jnp.where(work == m, eidx, float(ne)).min(axis=0, keepdims=True)
        sel = eidx == idx
        sels.append(sel)
        vals.append(m)
        work = jnp.where(sel, -jnp.inf, work)
    ex = [jnp.exp(v - vals[0]) for v in vals]
    den = ex[0] + ex[1] + ex[2] + ex[3]
    onehot = (sels[0] | sels[1] | sels[2] | sels[3]).astype(F32)
    cnt = onehot.sum(axis=1, keepdims=True).astype(I32)
    pc = ((cnt + (BF16_ROWS - 1)) >> 4) << 4
    pc_ref[0] = pc
    rank = jnp.dot(onehot.astype(BF16), tri_ref[...], preferred_element_type=F32)
    low = jnp.where(lax.broadcasted_iota(I32, (ne, LANES), 1) < lax.broadcasted_iota(I32, (ne, LANES), 0), 1.0, 0.0)
    pcb = jnp.concatenate([jnp.broadcast_to(pc.astype(F32), (ne, LANES)), jnp.zeros((LANES - ne, LANES), F32)], axis=0)
    off = jnp.dot(low.astype(BF16), pcb.astype(BF16), preferred_element_type=F32)[:, 0:1]
    posmat = rank + off
    for k in range(TOP_K):
        info_ref[k:k + 1, :] = jnp.where(sels[k], posmat, 0.0).sum(axis=0, keepdims=True)
        info_ref[TOP_K + k:TOP_K + k + 1, :] = ex[k] / den


def _router(logits_t, tb):
    ne, t = logits_t.shape
    nb = t // tb
    tri = jnp.triu(jnp.ones((tb, tb), BF16), k=1)
    return pl.pallas_call(
        _router_kernel,
        grid=(nb,),
        in_specs=[pl.BlockSpec((ne, tb), lambda i: (0, i)),
                  pl.BlockSpec((tb, tb), lambda i: (0, 0))],
        out_specs=[pl.BlockSpec((2 * TOP_K, tb), lambda i: (0, i)),
                   pl.BlockSpec((1, ne, 1), lambda i: (i, 0, 0))],
        out_shape=[jax.ShapeDtypeStruct((2 * TOP_K, t), F32),
                   jax.ShapeDtypeStruct((nb, ne, 1), I32)],
        compiler_params=_cparams(("arbitrary",)),
        name="router_top4",
    )(logits_t, tri)


def _local_chunks(tb, ne):
    rows = TOP_K * tb + ne * (BF16_ROWS - 1)
    return -(-rows // SORT_ROWS) * (SORT_ROWS // BF16_ROWS)


def _sort_plan(pc, tb, n_tokens):
    nb, ne = pc.shape
    nch = _local_chunks(tb, ne)
    n_tiles = (TOP_K * n_tokens + nb * ne * (BF16_ROWS - 1)) // FFN_TILE + ne
    loc_end = jnp.cumsum(pc, axis=1)
    loc_off = loc_end - pc
    tot = pc.sum(axis=0)
    tot_pad = ((tot + FFN_TILE - 1) // FFN_TILE) * FFN_TILE
    base_end = jnp.cumsum(tot_pad)
    base = base_end - tot_pad
    glob = base[None, :] + jnp.cumsum(pc, axis=0) - pc
    n_used = (base_end[-1] // FFN_TILE).astype(I32)
    tile_row = jnp.arange(n_tiles, dtype=I32) * FFN_TILE
    tile_expert = jnp.minimum((base_end[None, :] <= tile_row[:, None]).sum(axis=1), ne - 1).astype(I32)
    r = jnp.arange(nch, dtype=I32) * BF16_ROWS
    e_of = jnp.minimum((loc_end[:, None, :] <= r[None, :, None]).sum(axis=2), ne - 1)
    shift = glob - loc_off
    owner = e_of[:, :, None] == jnp.arange(ne, dtype=I32)[None, None, :]
    dst = jnp.where(owner, shift[:, None, :], 0).sum(axis=2) + r[None, :]
    n_valid = (loc_end[:, -1] // BF16_ROWS).astype(I32)
    valid = jnp.arange(nch, dtype=I32)[None, :] < n_valid[:, None]
    spare = n_tiles * (FFN_TILE // BF16_ROWS) + (jnp.arange(nb, dtype=I32)[:, None] % 2) * nch \
        + jnp.arange(nch, dtype=I32)[None, :]
    chunk_dst = jnp.where(valid, dst // BF16_ROWS, spare).astype(I32)
    chunk_src = jnp.where(valid, dst // BF16_ROWS, 0).astype(I32)
    tail_n = ((tot_pad - tot) // BF16_ROWS).astype(I32)
    tail_dst = ((base + tot) // BF16_ROWS).astype(I32)
    spare_tiles = -(-2 * nch * BF16_ROWS // FFN_TILE)
    return dict(nch=nch, n_tiles=n_tiles, spare_tiles=spare_tiles, chunk_dst=chunk_dst.reshape(-1),
                chunk_src=chunk_src.reshape(-1), n_used=n_used, tile_expert=tile_expert, tail_n=tail_n,
                tail_dst=tail_dst)


def _chunk_copy(buf, hbm, slot, c, dst_chunk, sem, to_hbm):
    v = buf.at[slot, pl.ds(pl.multiple_of(c * BF16_ROWS, BF16_ROWS), BF16_ROWS)]
    h = hbm.at[pl.ds(pl.multiple_of(dst_chunk * BF16_ROWS, BF16_ROWS), BF16_ROWS)]
    return pltpu.make_async_copy(v, h, sem) if to_hbm else pltpu.make_async_copy(h, v, sem)


def _local_row_ids(rt, tb):
    return lax.broadcasted_iota(I32, (rt, tb), 0).astype(F32).astype(BF16)


def _local_pos(pos_row, r0, rt):
    return jnp.clip(pos_row - float(r0), -1.0, float(rt)).astype(BF16)


def _dispatch_kernel(cd_ref, tn_ref, td_ref, nu_ref, h_ref, info_ref, hx_ref, infox_ref, xs_ref,
                     buf_ref, zero_ref, ztile_ref, sem, zsem, *, nch, rt, n_tiles, spare_tiles):
    b = pl.program_id(0)
    nb = pl.num_programs(0)
    slot = b % 2

    def send(c0, c1):
        for c in range(c0, c1):
            _chunk_copy(buf_ref, xs_ref, slot, c, cd_ref[b * nch + c], sem.at[slot], True).start()

    def sort_block(hr, ir):
        tb = hr.shape[0]
        h = hr[...]
        rid = _local_row_ids(rt, tb)
        one = jnp.ones((rt, tb), BF16)
        nrows = _local_chunks(tb, N_EXPERTS) * BF16_ROWS
        for r0 in range(0, nrows, rt):
            p = jnp.zeros((rt, tb), BF16)
            for k in range(TOP_K):
                p = jnp.where(rid == _local_pos(ir[k:k + 1, :], r0, rt), one, p)
            buf_ref[slot, r0:r0 + rt, :] = jnp.dot(p, h, preferred_element_type=F32).astype(BF16)
            send(r0 // BF16_ROWS, (r0 + rt) // BF16_ROWS)
        send(nrows // BF16_ROWS, nch)

    pl.when(b < nb - 1)(lambda: sort_block(h_ref, info_ref))
    pl.when(b == nb - 1)(lambda: sort_block(hx_ref, infox_ref))

    def drain(step, sl):
        for c in range(nch):
            _chunk_copy(buf_ref, xs_ref, sl, c, cd_ref[step * nch + c], sem.at[sl], True).wait()

    @pl.when(b > 0)
    def _():
        drain(b - 1, 1 - slot)

    @pl.when(b == nb - 1)
    def _():
        drain(b, slot)
        zero_ref[...] = jnp.zeros_like(zero_ref)
        for e in range(N_EXPERTS):
            def zcopy(c, e=e):
                dst = pl.multiple_of((td_ref[e] + c) * BF16_ROWS, BF16_ROWS)
                return pltpu.make_async_copy(zero_ref, xs_ref.at[pl.ds(dst, BF16_ROWS)], zsem)

            lax.fori_loop(0, tn_ref[e], lambda c, _: (zcopy(c).start(), 0)[1], 0)
            lax.fori_loop(0, tn_ref[e], lambda c, _: (zcopy(c).wait(), 0)[1], 0)

        ztile_ref[...] = jnp.zeros_like(ztile_ref)

        def ztile(i):
            dst = pl.multiple_of(i * FFN_TILE, FFN_TILE)
            return pltpu.make_async_copy(ztile_ref, xs_ref.at[pl.ds(dst, FFN_TILE)], zsem)

        lax.fori_loop(nu_ref[0], n_tiles + spare_tiles, lambda i, _: (ztile(i).start(), 0)[1], 0)
        lax.fori_loop(nu_ref[0], n_tiles + spare_tiles, lambda i, _: (ztile(i).wait(), 0)[1], 0)


def _dispatch(h2, info, h2x, infox, plan, tb):
    t, d = h2.shape
    tx = h2x.shape[0]
    nb = t // tb
    nch = plan["nch"]
    rows = (plan["n_tiles"] + plan["spare_tiles"]) * FFN_TILE
    assert (nch * BF16_ROWS) % SORT_ROWS == 0 and tx <= tb
    kern = functools.partial(_dispatch_kernel, nch=nch, rt=SORT_ROWS, n_tiles=plan["n_tiles"],
                             spare_tiles=plan["spare_tiles"])
    grid_spec = pltpu.PrefetchScalarGridSpec(
        num_scalar_prefetch=4,
        grid=(nb + 1,),
        in_specs=[pl.BlockSpec((tb, d), lambda i, *_: (jnp.minimum(i, nb - 1), 0)),
                  pl.BlockSpec((2 * TOP_K, tb), lambda i, *_: (0, jnp.minimum(i, nb - 1))),
                  pl.BlockSpec((tx, d), lambda i, *_: (0, 0)),
                  pl.BlockSpec((2 * TOP_K, tx), lambda i, *_: (0, 0))],
        out_specs=pl.BlockSpec(memory_space=pl.ANY),
        scratch_shapes=[pltpu.VMEM((2, nch * BF16_ROWS, d), BF16),
                        pltpu.VMEM((BF16_ROWS, d), BF16),
                        pltpu.VMEM((FFN_TILE, d), BF16),
                        pltpu.SemaphoreType.DMA((2,)),
                        pltpu.SemaphoreType.DMA],
    )
    return pl.pallas_call(
        kern,
        grid_spec=grid_spec,
        out_shape=jax.ShapeDtypeStruct((rows, d), BF16),
        compiler_params=_cparams(("arbitrary",), VMEM_LIMIT),
        name="moe_dispatch",
    )(plan["chunk_dst"], plan["tail_n"], plan["tail_dst"], plan["n_used"].reshape(1), h2, info, h2x, infox)


def _ffn_kernel(te_ref, nu_ref, x_ref, wgu_ref, bgu_ref, wd_ref, bd_ref, perm_ref, y_ref, wgu_s, wd_s):
    i = pl.program_id(0)
    f = wd_s.shape[0]
    prev = te_ref[jnp.maximum(i - 1, 0)]
    new_expert = (i == 0) | (te_ref[i] != prev)

    @pl.when((i < nu_ref[0]) & new_expert)
    def _():
        for c in range(0, 2 * f, 2 * LANES):
            blk = wgu_ref[0, :, c:c + 2 * LANES].astype(BF16)
            wgu_s[:, c:c + 2 * LANES] = jnp.dot(blk, perm_ref[...], preferred_element_type=F32).astype(BF16)
        wd_s[...] = wd_ref[0].astype(BF16)

    @pl.when(i < nu_ref[0])
    def _():
        x = x_ref[...]
        acts = []
        for c in range(0, 2 * f, 2 * LANES):
            gu = jnp.dot(x, wgu_s[:, c:c + 2 * LANES], preferred_element_type=F32) + bgu_ref[0, :, c:c + 2 * LANES]
            glu = jnp.minimum(gu[:, :LANES], SWIGLU_LIMIT)
            lin = jnp.clip(gu[:, LANES:], -SWIGLU_LIMIT, SWIGLU_LIMIT)
            acts.append((glu * jax.nn.sigmoid(SWIGLU_ALPHA * glu) * (lin + 1.0)).astype(BF16))
        act = jnp.concatenate(acts, axis=1)
        y_ref[...] = (jnp.dot(act, wd_s[...], preferred_element_type=F32) + bd_ref[0]).astype(BF16)


def _ffn(xs, plan, w_gate_up, b_gu_perm, w_down, b_down):
    rows, d = xs.shape
    ne, _, f2 = w_gate_up.shape
    f = f2 // 2
    n_tiles = plan["n_tiles"]
    src = jnp.concatenate([jnp.arange(LANES) * 2, jnp.arange(LANES) * 2 + 1])
    perm = (jnp.arange(2 * LANES)[:, None] == src[None, :]).astype(BF16)

    def xi(i, te, nu):
        return (jnp.minimum(i, nu[0] - 1), 0)

    def wi(i, te, nu):
        return (te[jnp.minimum(i, nu[0] - 1)], 0, 0)

    grid_spec = pltpu.PrefetchScalarGridSpec(
        num_scalar_prefetch=2,
        grid=(n_tiles,),
        in_specs=[pl.BlockSpec((FFN_TILE, d), xi),
                  pl.BlockSpec((1, d, f2), wi),
                  pl.BlockSpec((1, 1, f2), wi),
                  pl.BlockSpec((1, f, d), wi),
                  pl.BlockSpec((1, 1, d), wi),
                  pl.BlockSpec((2 * LANES, 2 * LANES), lambda i, te, nu: (0, 0))],
        out_specs=pl.BlockSpec((FFN_TILE, d), xi),
        scratch_shapes=[pltpu.VMEM((d, f2), BF16), pltpu.VMEM((f, d), BF16)],
    )
    return pl.pallas_call(
        _ffn_kernel,
        grid_spec=grid_spec,
        out_shape=jax.ShapeDtypeStruct((rows, d), BF16),
        input_output_aliases={2: 0},
        compiler_params=_cparams(("arbitrary",), VMEM_LIMIT),
        name="moe_expert_ffn",
    )(plan["tile_expert"], plan["n_used"].reshape(1), xs, w_gate_up, b_gu_perm.reshape(ne, 1, f2), w_down,
      b_down.reshape(ne, 1, d), perm)


def _combine_kernel(cd_ref, ys_ref, info_ref, x1_ref, mod_ref, infox_ref, x1x_ref, modx_ref, g_ref,
                    o_ref, ox_ref, buf_ref, w_ref, sem, *, nch, rt):
    b = pl.program_id(0)
    nb = pl.num_programs(0)
    slot = b % 2

    def fetch(step, sl, c0, c1):
        for c in range(c0, c1):
            _chunk_copy(buf_ref, ys_ref, sl, c, cd_ref[step * nch + c], sem.at[sl], False).start()

    @pl.when(b == 0)
    def _():
        fetch(0, 0, 0, nch)

    def unsort_block(ir, x1r, modr, outr, prefetch_next):
        tb = x1r.shape[0]
        nbat = modr.shape[0]
        rb = tb // nbat
        nrows = _local_chunks(tb, N_EXPERTS) * BF16_ROWS
        per_tile = nch // (nrows // rt)
        rid = _local_row_ids(rt, tb)
        for j, r0 in enumerate(range(0, nrows, rt)):
            w = jnp.zeros((rt, tb), BF16)
            for k in range(TOP_K):
                gate = jnp.broadcast_to(ir[TOP_K + k:TOP_K + k + 1, :].astype(BF16), (rt, tb))
                w = jnp.where(rid == _local_pos(ir[k:k + 1, :], r0, rt), gate, w)
            w_ref[r0:r0 + rt, 0:tb] = w
            if prefetch_next:
                fetch(b + 1, 1 - slot, j * per_tile, (j + 1) * per_tile)

        for c in range(nch):
            _chunk_copy(buf_ref, ys_ref, slot, c, cd_ref[b * nch + c], sem.at[slot], False).wait()

        moe = lax.dot_general(w_ref[0:nrows, 0:tb], buf_ref[slot, 0:nrows, :], (((0,), (0,)), ((), ())),
                              preferred_element_type=F32)
        out = _rms(moe, g_ref[...])
        for s in range(nbat):
            rs = slice(s * rb, (s + 1) * rb)
            outr[rs, :] = x1r[rs, :] + modr[s, 5:6, :] * out[rs, :]

    pl.when(b < nb - 1)(lambda: unsort_block(info_ref, x1_ref, mod_ref, o_ref, True))
    pl.when(b == nb - 1)(lambda: unsort_block(infox_ref, x1x_ref, modx_ref, ox_ref, False))


def _combine(ys, info, x1, mod, infox, x1x, modx, g_post_ffn, plan, tb, l):
    t, d = x1.shape
    tx = x1x.shape[0]
    nb = t // tb
    nch = plan["nch"]
    assert l % tb == 0
    clamp = lambda i: jnp.minimum(i, nb - 1)
    kern = functools.partial(_combine_kernel, nch=nch, rt=SORT_ROWS)
    grid_spec = pltpu.PrefetchScalarGridSpec(
        num_scalar_prefetch=1,
        grid=(nb + 1,),
        in_specs=[pl.BlockSpec(memory_space=pl.ANY),
                  pl.BlockSpec((2 * TOP_K, tb), lambda i, *_: (0, clamp(i))),
                  pl.BlockSpec((tb, d), lambda i, *_: (clamp(i), 0)),
                  pl.BlockSpec((1, 6, d), lambda i, *_: (clamp(i) // (l // tb), 0, 0)),
                  pl.BlockSpec((2 * TOP_K, tx), lambda i, *_: (0, 0)),
                  pl.BlockSpec((tx, d), lambda i, *_: (0, 0)),
                  pl.BlockSpec(modx.shape, lambda i, *_: (0, 0, 0)),
                  pl.BlockSpec((1, d), lambda i, *_: (0, 0))],
        out_specs=[pl.BlockSpec((tb, d), lambda i, *_: (clamp(i), 0)),
                   pl.BlockSpec((tx, d), lambda i, *_: (0, 0))],
        scratch_shapes=[pltpu.VMEM((2, nch * BF16_ROWS, d), BF16), pltpu.VMEM((nch * BF16_ROWS, tb), BF16),
                        pltpu.SemaphoreType.DMA((2,))],
    )
    return pl.pallas_call(
        kern,
        grid_spec=grid_spec,
        out_shape=[jax.ShapeDtypeStruct((t, d), F32), jax.ShapeDtypeStruct((tx, d), F32)],
        compiler_params=_cparams(("arbitrary",), VMEM_LIMIT),
        name="moe_combine",
    )(plan["chunk_src"], ys, info, x1, mod, infox, x1x, modx, g_post_ffn.reshape(1, d))


def _moe(main, second, l, g_post_ffn, w_gate_up, b_gu_perm, w_down, b_down):
    h2, logits, x1, mod = main
    h2x, logitsx, x1x, modx = second
    t = h2.shape[0]
    tx = h2x.shape[0]
    tb = ROW_TILE
    info, pc = _router(logits[:, :N_EXPERTS].T, tb)
    infox, pcx = _router(logitsx[:, :N_EXPERTS].T, tx)
    pc_all = jnp.concatenate([pc.reshape(t // tb, N_EXPERTS), pcx.reshape(1, N_EXPERTS)], axis=0)
    plan = _sort_plan(pc_all, tb, t + tx)
    xs = _dispatch(h2, info, h2x, infox, plan, tb)
    ys = _ffn(xs, plan, w_gate_up, b_gu_perm, w_down, b_down)
    return _combine(ys, info, x1, mod, infox, x1x, modx, g_post_ffn, plan, tb, l)


def _stream(x, mod, wts, attn_fn, s0_packs):
    b, l, d = x.shape
    q, k_att, v_att, u, kf, vf = _inproj(x, mod, wts["g_pre_mix"], wts["w_in"], pad_front=attn_fn is None)
    if attn_fn is None:
        attn = _attn_prompt(q, k_att, v_att, wts["table_prompt"])
    else:
        attn = attn_fn(q, k_att, v_att)
    ssm, sf = _s5_scan(u, wts["bb"], wts["cc"], wts["dvec"], wts["are"], wts["aim"], s0_packs)
    x1, h2, logits = _outproj(attn, ssm, x, mod, wts["w_glu"], wts["b_glu"], wts["w_out"], wts["g_post_mix"],
                              wts["g_pre_ffn"], wts["w_router"], wts["b_router"])
    return (h2, logits, x1.reshape(b * l, d), mod), kf, vf, sf


def kernel(x_prompt, x_sample, cache_k, cache_v, state_ssm_re, state_ssm_im, c_prompt, c_sample, w_ada, b_ada, g_pre_mix, g_post_mix, g_pre_ffn, g_post_ffn, w_in, rel_bias, ssm_a_re, ssm_a_im, ssm_log_dt, ssm_b_re, ssm_b_im, ssm_c_re, ssm_c_im, ssm_d, w_glu, b_glu, w_out, w_router, b_router, w_gate_up, b_gate_up, w_down, b_down):
    depth = w_ada.shape[0]
    assert depth == 1, "single-layer trunk"
    bp, lp, d = x_prompt.shape
    bs, ls, _ = x_sample.shape
    n_cache = cache_k.shape[2]
    past_len = PAST_LEN
    g, p = ssm_a_re.shape[1:]
    lyr = 0

    mod = _ada(jnp.concatenate([c_prompt, c_sample], axis=0), w_ada[lyr], b_ada[lyr])
    mod = mod.reshape(bp + bs, 6, d)

    abre, abim, bbre, bbim = _s5_prep(ssm_a_re[lyr], ssm_a_im[lyr], ssm_log_dt[lyr], ssm_b_re[lyr], ssm_b_im[lyr])
    bb, cc, dvec, are, aim = _s5_pack_weights(bbre, bbim, ssm_c_re[lyr], ssm_c_im[lyr], ssm_d[lyr], abre, abim)

    f2 = w_gate_up.shape[-1]
    b_gu_perm = b_gate_up[lyr].reshape(N_EXPERTS, f2 // (2 * LANES), LANES, 2)
    b_gu_perm = jnp.transpose(b_gu_perm, (0, 1, 3, 2)).reshape(N_EXPERTS, f2)

    rb = rel_bias[lyr]
    wts = dict(
        g_pre_mix=g_pre_mix[lyr], g_post_mix=g_post_mix[lyr], g_pre_ffn=g_pre_ffn[lyr], g_post_ffn=g_post_ffn[lyr],
        w_in=w_in[lyr].astype(BF16), w_glu=w_glu[lyr].astype(BF16), b_glu=b_glu[lyr], w_out=w_out[lyr].astype(BF16),
        w_router=jnp.pad(w_router[lyr], ((0, 0), (0, LANES - N_EXPERTS))).astype(BF16),
        b_router=jnp.pad(b_router[lyr], (0, LANES - N_EXPERTS)),
        w_gate_up=w_gate_up[lyr], b_gu_perm=b_gu_perm, w_down=w_down[lyr], b_down=b_down[lyr],
        bb=bb, cc=cc, dvec=dvec, are=are, aim=aim,
        table_prompt=_bias_table(rb, q0=WIN_ROWS, k0=0, nq=CHUNK, nk=Q_TILE + WIN_ROWS, strips=Q_TILE // CHUNK),
    )

    zero_state = jnp.zeros((g // SSM_PACK, bp, 2 * SSM_PACK * p), F32)
    ffn_in_p, kf_p, vf_p, sf_p = _stream(x_prompt, mod[:bp], wts, None, zero_state)

    tabc = _bias_table(rb, q0=past_len, k0=past_len - n_cache, nq=ls, nk=n_cache)
    tabn = _bias_table(rb, q0=past_len, k0=past_len, nq=ls, nk=ls)
    ck = cache_k[lyr].reshape(bs, n_cache, D_ATTN)
    cv = cache_v[lyr].reshape(bs, n_cache, D_ATTN)
    attn_s = lambda q, k, v: _attn_sample(q, k, v, ck, cv, tabc, tabn)
    s0 = _state_to_packs(state_ssm_re[lyr], state_ssm_im[lyr])
    ffn_in_s, kf_s, vf_s, sf_s = _stream(x_sample, mod[bp:], wts, attn_s, s0)

    y_p, y_s = _moe(ffn_in_p, ffn_in_s, lp, wts["g_post_ffn"], wts["w_gate_up"], wts["b_gu_perm"], wts["w_down"],
                    wts["b_down"])
    y_p = y_p.reshape(bp, lp, d)
    y_s = y_s.reshape(bs, ls, d)

    def heads(a):
        return a.reshape(1, a.shape[0], a.shape[1], N_HEADS, HEAD_DIM)

    sre_p, sim_p = _packs_to_state(sf_p, g, p)
    sre_s, sim_s = _packs_to_state(sf_s, g, p)
    return (y_p, y_s, heads(kf_p), heads(vf_p), sre_p[None], sim_p[None],
            heads(kf_s), heads(vf_s), sre_s[None], sim_s[None])
```

```python
import functools
import math

import jax
import jax.numpy as jnp
from jax import lax
from jax.experimental import pallas as pl
from jax.experimental.pallas import tpu as pltpu

F32 = jnp.float32
BF16 = jnp.bfloat16
I32 = jnp.int32
HIGHEST = lax.Precision.HIGHEST

CHUNK = 64
CHUNK_LOG2 = 6
N_PAST_CHUNKS = 8
PAST_LEN = 2048
WIN_ROWS = N_PAST_CHUNKS * CHUNK
MAX_REL = 64
N_REL = 2 * MAX_REL + 1
N_HEADS = 8
HEAD_DIM = 64
D_ATTN = N_HEADS * HEAD_DIM
SSM_GROUP = 16
SSM_STATE = 64
N_EXPERTS = 32
TOP_K = 4
SWIGLU_LIMIT = 7.0
SWIGLU_ALPHA = 1.702
RMS_EPS = 1e-6
NEG_INF = -1e30
LOG2E = math.log2(math.e)

LANES = 128
BF16_ROWS = 16
VMEM_LIMIT = 56 * 1024 * 1024

ROW_TILE = 512
Q_TILE = 256
ATTN_QSUB = 128
ATTN_KBLK = 256
SSM_PACK = 8
SSM_TIME = 64
SSM_TSUB = 16
FFN_TILE = 512
SORT_ROWS = 256


def _cparams(sem, vmem=None):
    return pltpu.CompilerParams(dimension_semantics=sem, vmem_limit_bytes=vmem)


def _rms(x, g):
    return x * lax.rsqrt(jnp.mean(x * x, axis=-1, keepdims=True) + RMS_EPS) * g


def _ada_kernel(c_ref, w_ref, b_ref, o_ref):
    c = c_ref[...]
    s = c * jax.nn.sigmoid(c)
    o_ref[...] = jnp.dot(s, w_ref[...], precision=HIGHEST, preferred_element_type=F32) + b_ref[...]


def _ada(c, w_ada, b_ada):
    n, d = c.shape
    nout = w_ada.shape[1]
    return pl.pallas_call(
        _ada_kernel,
        grid=(nout // d,),
        in_specs=[pl.BlockSpec((n, d), lambda j: (0, 0)),
                  pl.BlockSpec((d, d), lambda j: (0, j)),
                  pl.BlockSpec((1, d), lambda j: (0, j))],
        out_specs=pl.BlockSpec((n, d), lambda j: (0, j)),
        out_shape=jax.ShapeDtypeStruct((n, nout), F32),
        compiler_params=_cparams(("arbitrary",), VMEM_LIMIT),
        name="ada_mod",
    )(c, w_ada, b_ada.reshape(1, nout))


def _inproj_kernel(x_ref, mod_ref, g_ref, w_ref, q_ref, k_ref, v_ref, u_ref, kf_ref, vf_ref, *, first_keep, pad):
    j = pl.program_id(1)

    def compute():
        x = x_ref[0]
        h = _rms(x, g_ref[...]) * (1.0 + mod_ref[0, 1:2, :]) + mod_ref[0, 0:1, :]
        proj = jnp.dot(h.astype(BF16), w_ref[...], preferred_element_type=F32)
        q_ref[0] = (proj[:, :D_ATTN] * (HEAD_DIM ** -0.5 * LOG2E)).astype(BF16)
        k_ref[0] = proj[:, D_ATTN:2 * D_ATTN].astype(BF16)
        v_ref[0] = proj[:, 2 * D_ATTN:3 * D_ATTN].astype(BF16)
        u_ref[0] = proj[:, 3 * D_ATTN:].astype(BF16)

        @pl.when(j >= first_keep + pad)
        def _():
            kf_ref[0] = proj[:, D_ATTN:2 * D_ATTN]
            vf_ref[0] = proj[:, 2 * D_ATTN:3 * D_ATTN]

    if pad:
        @pl.when(j == 0)
        def _():
            k_ref[...] = jnp.zeros_like(k_ref)
            v_ref[...] = jnp.zeros_like(v_ref)

        pl.when(j > 0)(compute)
    else:
        compute()


def _inproj(x, mod, g, w_in_bf, *, pad_front):
    b, l, d = x.shape
    tm = min(ROW_TILE, l)
    nt = l // tm
    keep = min(WIN_ROWS, l)
    first_keep = (l - keep) // tm
    pad = 1 if pad_front else 0
    d_ssm = w_in_bf.shape[1] - 3 * D_ATTN
    assert l % tm == 0 and keep % tm == 0 and (not pad_front or tm == WIN_ROWS)

    def xi(bi, j):
        return (bi, jnp.maximum(j - pad, 0), 0)

    def kfi(bi, j):
        return (bi, jnp.maximum(j - pad - first_keep, 0), 0)

    kern = functools.partial(_inproj_kernel, first_keep=first_keep, pad=pad)
    return pl.pallas_call(
        kern,
        grid=(b, nt + pad),
        in_specs=[pl.BlockSpec((1, tm, d), xi),
                  pl.BlockSpec((1, 6, d), lambda bi, j: (bi, 0, 0)),
                  pl.BlockSpec((1, d), lambda bi, j: (0, 0)),
                  pl.BlockSpec(w_in_bf.shape, lambda bi, j: (0, 0))],
        out_specs=[pl.BlockSpec((1, tm, D_ATTN), xi),
                   pl.BlockSpec((1, tm, D_ATTN), lambda bi, j: (bi, j, 0)),
                   pl.BlockSpec((1, tm, D_ATTN), lambda bi, j: (bi, j, 0)),
                   pl.BlockSpec((1, tm, d_ssm), xi),
                   pl.BlockSpec((1, tm, D_ATTN), kfi),
                   pl.BlockSpec((1, tm, D_ATTN), kfi)],
        out_shape=[jax.ShapeDtypeStruct((b, l, D_ATTN), BF16),
                   jax.ShapeDtypeStruct((b, l + pad * tm, D_ATTN), BF16),
                   jax.ShapeDtypeStruct((b, l + pad * tm, D_ATTN), BF16),
                   jax.ShapeDtypeStruct((b, l, d_ssm), BF16),
                   jax.ShapeDtypeStruct((b, keep, D_ATTN), F32),
                   jax.ShapeDtypeStruct((b, keep, D_ATTN), F32)],
        compiler_params=_cparams(("arbitrary", "arbitrary"), VMEM_LIMIT),
        name="prenorm_inproj",
    )(x, mod, g.reshape(1, d), w_in_bf)


def _bias_kernel(rb_ref, o_ref, *, q0, k0, nq, nk, strips):
    h = pl.program_id(0)
    qpos = q0 + lax.broadcasted_iota(I32, (nq, nk), 0)
    kpos = k0 + lax.broadcasted_iota(I32, (nq, nk), 1)
    rel = jnp.clip(kpos - qpos, -MAX_REL, MAX_REL) + MAX_REL
    qc = qpos >> CHUNK_LOG2
    kc = kpos >> CHUNK_LOG2
    allowed = (kc <= qc) & (kc >= qc - N_PAST_CHUNKS)

    def body(r, acc):
        return jnp.where(rel == r, rb_ref[r, h], acc)

    val = lax.fori_loop(0, N_REL, body, jnp.zeros((nq, nk), F32))
    strip = jnp.where(allowed, val * LOG2E, NEG_INF)
    o_ref[0, 0:nq, :] = strip
    for c in range(1, strips):
        o_ref[0, c * nq:(c + 1) * nq, :] = pltpu.roll(strip, c * CHUNK, axis=1)


def _bias_table(rel_bias, *, q0, k0, nq, nk, strips=1):
    kern = functools.partial(_bias_kernel, q0=q0, k0=k0, nq=nq, nk=nk, strips=strips)
    return pl.pallas_call(
        kern,
        grid=(N_HEADS,),
        in_specs=[pl.BlockSpec(memory_space=pltpu.SMEM)],
        out_specs=pl.BlockSpec((1, strips * nq, nk), lambda h: (h, 0, 0)),
        out_shape=jax.ShapeDtypeStruct((N_HEADS, strips * nq, nk), F32),
        compiler_params=_cparams(("arbitrary",)),
        name="bias_table",
    )(rel_bias)


def _softmax_pv(scores, values):
    m = scores[0].max(axis=-1, keepdims=True)
    for s in scores[1:]:
        m = jnp.maximum(m, s.max(axis=-1, keepdims=True))
    acc = None
    den = None
    for s, v in zip(scores, values):
        p = jnp.exp2(s - m)
        d = p.sum(axis=-1, keepdims=True)
        o = jnp.dot(p.astype(BF16), v, preferred_element_type=F32)
        acc = o if acc is None else acc + o
        den = d if den is None else den + d
    return acc / den


def _attn_prompt_kernel(q_ref, k_ref, v_ref, tab_ref, o_ref, *, tq, band):
    qi = pl.program_id(1)
    t0 = pl.multiple_of(qi * tq, tq)
    lane = lax.broadcasted_iota(I32, (1, LANES), 1)

    def body(mask_front):
        if mask_front:
            col = lax.broadcasted_iota(I32, (1, band), 1)
            colmask = jnp.where(col + t0 >= WIN_ROWS, 0.0, NEG_INF).astype(F32)
        for hp in range(N_HEADS // 2):
            ls = slice(hp * LANES, (hp + 1) * LANES)
            for q0 in range(0, tq, ATTN_QSUB):
                rows = slice(q0, q0 + ATTN_QSUB)
                q2 = q_ref[0, rows, ls]
                outs = []
                for a in range(2):
                    qa = jnp.where((lane < HEAD_DIM) == (a == 0), q2, jnp.zeros_like(q2))
                    m = den = acc = None
                    for k0 in range(0, band, ATTN_KBLK):
                        if k0 + ATTN_KBLK <= q0 or k0 >= q0 + ATTN_QSUB + WIN_ROWS:
                            continue
                        cols = slice(k0, k0 + ATTN_KBLK)
                        kk = k_ref[0, pl.ds(t0 + k0, ATTN_KBLK), ls]
                        vv = v_ref[0, pl.ds(t0 + k0, ATTN_KBLK), ls]
                        s = lax.dot_general(qa, kk, (((1,), (1,)), ((), ())), preferred_element_type=F32)
                        s = s + tab_ref[2 * hp + a, rows, cols]
                        if mask_front:
                            s = s + colmask[:, cols]
                        mb = s.max(axis=-1, keepdims=True)
                        if m is None:
                            m = mb
                            p = jnp.exp2(s - m)
                            den = p.sum(axis=-1, keepdims=True)
                            acc = jnp.dot(p.astype(BF16), vv, preferred_element_type=F32)
                        else:
                            m_new = jnp.maximum(m, mb)
                            alpha = jnp.exp2(m - m_new)
                            p = jnp.exp2(s - m_new)
                            den = alpha * den + p.sum(axis=-1, keepdims=True)
                            acc = alpha * acc + jnp.dot(p.astype(BF16), vv, preferred_element_type=F32)
                            m = m_new
                    outs.append(acc / den)
                o_ref[0, rows, ls] = jnp.where(lane < HEAD_DIM, outs[0], outs[1]).astype(BF16)

    pl.when(t0 < WIN_ROWS)(lambda: body(True))
    pl.when(t0 >= WIN_ROWS)(lambda: body(False))


def _attn_prompt(q, k_pad, v_pad, table):
    b, l, _ = q.shape
    tq = Q_TILE
    band = tq + WIN_ROWS
    kern = functools.partial(_attn_prompt_kernel, tq=tq, band=band)
    return pl.pallas_call(
        kern,
        grid=(b, l // tq),
        in_specs=[pl.BlockSpec((1, tq, D_ATTN), lambda bi, i: (bi, i, 0)),
                  pl.BlockSpec((1, l + WIN_ROWS, D_ATTN), lambda bi, i: (bi, 0, 0)),
                  pl.BlockSpec((1, l + WIN_ROWS, D_ATTN), lambda bi, i: (bi, 0, 0)),
                  pl.BlockSpec(table.shape, lambda bi, i: (0, 0, 0))],
        out_specs=pl.BlockSpec((1, tq, D_ATTN), lambda bi, i: (bi, i, 0)),
        out_shape=jax.ShapeDtypeStruct((b, l, D_ATTN), BF16),
        compiler_params=_cparams(("arbitrary", "arbitrary"), VMEM_LIMIT),
        name="attn_prompt",
    )(q, k_pad, v_pad, table)


def _attn_sample_kernel(q_ref, k_ref, v_ref, ck_ref, cv_ref, tab_ref, o_ref):
    lane = lax.broadcasted_iota(I32, (1, LANES), 1)
    nc = ck_ref.shape[1]
    for hp in range(N_HEADS // 2):
        ls = slice(hp * LANES, (hp + 1) * LANES)
        q2 = q_ref[0, :, ls]
        kn = k_ref[0, :, ls]
        vn = v_ref[0, :, ls]
        kc = ck_ref[0, :, ls].astype(BF16)
        vc = cv_ref[0, :, ls].astype(BF16)
        outs = []
        for a in range(2):
            h = 2 * hp + a
            qa = jnp.where((lane < HEAD_DIM) == (a == 0), q2, jnp.zeros_like(q2))
            sc = lax.dot_general(qa, kc, (((1,), (1,)), ((), ())), preferred_element_type=F32) + tab_ref[h, :, :nc]
            sn = lax.dot_general(qa, kn, (((1,), (1,)), ((), ())), preferred_element_type=F32) + tab_ref[h, :, nc:]
            outs.append(_softmax_pv([sc, sn], [vc, vn]))
        o_ref[0, :, ls] = jnp.where(lane < HEAD_DIM, outs[0], outs[1]).astype(BF16)


def _attn_sample(q, k, v, cache_k, cache_v, table):
    b, s, _ = q.shape
    nc = cache_k.shape[1]
    return pl.pallas_call(
        _attn_sample_kernel,
        grid=(b,),
        in_specs=[pl.BlockSpec((1, s, D_ATTN), lambda bi: (bi, 0, 0)),
                  pl.BlockSpec((1, s, D_ATTN), lambda bi: (bi, 0, 0)),
                  pl.BlockSpec((1, s, D_ATTN), lambda bi: (bi, 0, 0)),
                  pl.BlockSpec((1, nc, D_ATTN), lambda bi: (bi, 0, 0)),
                  pl.BlockSpec((1, nc, D_ATTN), lambda bi: (bi, 0, 0)),
                  pl.BlockSpec(table.shape, lambda bi: (0, 0, 0))],
        out_specs=pl.BlockSpec((1, s, D_ATTN), lambda bi: (bi, 0, 0)),
        out_shape=jax.ShapeDtypeStruct((b, s, D_ATTN), BF16),
        compiler_params=_cparams(("arbitrary",), VMEM_LIMIT),
        name="attn_sample",
    )(q, k, v, cache_k, cache_v, table)


def _s5_prep_kernel(are_ref, aim_ref, ldt_ref, bre_ref, bim_ref, abre_ref, abim_ref, bbre_ref, bbim_ref):
    a_re = are_ref[...]
    a_im = aim_ref[...]
    dt = jnp.exp(ldt_ref[...])
    mag = jnp.exp(dt * a_re)
    abar_re = mag * jnp.cos(dt * a_im)
    abar_im = mag * jnp.sin(dt * a_im)
    den = a_re * a_re + a_im * a_im
    num_re = abar_re - 1.0
    coef_re = (num_re * a_re + abar_im * a_im) / den
    coef_im = (abar_im * a_re - num_re * a_im) / den
    abre_ref[...] = abar_re
    abim_ref[...] = abar_im
    b_re = bre_ref[...]
    b_im = bim_ref[...]
    bbre_ref[...] = coef_re * b_re - coef_im * b_im
    bbim_ref[...] = coef_re * b_im + coef_im * b_re


def _s5_prep(a_re, a_im, log_dt, b_re, b_im):
    g, p = a_re.shape
    bt_re = jnp.transpose(b_re, (0, 2, 1))
    bt_im = jnp.transpose(b_im, (0, 2, 1))
    i = bt_re.shape[1]
    sds = jax.ShapeDtypeStruct
    return pl.pallas_call(
        _s5_prep_kernel,
        out_shape=[sds((g, 1, p), F32), sds((g, 1, p), F32), sds((g, i, p), F32), sds((g, i, p), F32)],
        name="s5_discretise",
    )(a_re.reshape(g, 1, p), a_im.reshape(g, 1, p), log_dt.reshape(g, 1, 1), bt_re, bt_im)


def _s5_scan_kernel(u_ref, perm_ref, permt_ref, bb_ref, cc_ref, d_ref, are_ref, aim_ref, s0_ref, y_ref, sf_ref,
                    utb_ref, ytb_ref, bu_ref, xb_ref, st_ref, *, nb, tc, cin, tsub):
    t_step = pl.program_id(1)
    half = bu_ref.shape[2] // 2
    packed_rows = nb % BF16_ROWS == 0
    sub = nb * tsub

    @pl.when(t_step == 0)
    def _():
        st_ref[...] = s0_ref[...]

    for s in range(tc // tsub):
        blk = u_ref[:, s * tsub:(s + 1) * tsub, :].reshape(sub, 2 * cin)
        utb_ref[s * sub:(s + 1) * sub, :] = jnp.dot(perm_ref[...], blk, preferred_element_type=F32).astype(BF16)

    us = []
    for pk in range(2):
        u = utb_ref[:, pk * cin:(pk + 1) * cin]
        us.append(u)
        bu_ref[pk] = jnp.dot(u, bb_ref[pk], preferred_element_type=F32)
    for pk in range(2):
        a_re = jnp.broadcast_to(are_ref[pk], (nb, half))
        a_im = jnp.broadcast_to(aim_ref[pk], (nb, half))
        x_re = st_ref[pk, :, 0:half]
        x_im = st_ref[pk, :, half:2 * half]
        for t in range(tc):
            rows = slice(t * nb, (t + 1) * nb)
            n_re = a_re * x_re - a_im * x_im + bu_ref[pk, rows, 0:half]
            n_im = a_re * x_im + a_im * x_re + bu_ref[pk, rows, half:2 * half]
            x_re, x_im = n_re, n_im
            if packed_rows:
                xb_ref[pk, rows, 0:half] = x_re.astype(BF16)
                xb_ref[pk, rows, half:2 * half] = x_im.astype(BF16)
            else:
                bu_ref[pk, rows, 0:half] = x_re
                bu_ref[pk, rows, half:2 * half] = x_im
        st_ref[pk, :, 0:half] = x_re
        st_ref[pk, :, half:2 * half] = x_im
    sf_ref[...] = st_ref[...]
    for pk in range(2):
        xs = xb_ref[pk] if packed_rows else bu_ref[pk].astype(BF16)
        y = jnp.dot(xs, cc_ref[pk], preferred_element_type=F32)
        y = y + d_ref[pk] * us[pk].astype(F32)
        ytb_ref[:, pk * cin:(pk + 1) * cin] = jax.nn.gelu(y).astype(BF16)
    for s in range(tc // tsub):
        blk = jnp.dot(permt_ref[...], ytb_ref[s * sub:(s + 1) * sub, :], preferred_element_type=F32)
        y_ref[:, s * tsub:(s + 1) * tsub, :] = blk.astype(BF16).reshape(nb, tsub, 2 * cin)


def _s5_scan(u, bb, cc, dvec, abar_re, abar_im, s0):
    nb, l, d_ssm = u.shape
    npk = bb.shape[0]
    cin = bb.shape[1]
    two_half = bb.shape[2]
    tc = min(SSM_TIME, l)
    tsub = min(SSM_TSUB, tc)
    rows = tc * nb
    assert npk % 2 == 0 and tc % tsub == 0
    src = (jnp.arange(nb * tsub) % nb) * tsub + jnp.arange(nb * tsub) // nb
    perm = (src[:, None] == jnp.arange(nb * tsub)[None, :]).astype(BF16)
    kern = functools.partial(_s5_scan_kernel, nb=nb, tc=tc, cin=cin, tsub=tsub)
    pair = lambda p, t: (p, 0, 0)
    return pl.pallas_call(
        kern,
        grid=(npk // 2, l // tc),
        in_specs=[pl.BlockSpec((nb, tc, 2 * cin), lambda p, t: (0, t, p)),
                  pl.BlockSpec(perm.shape, lambda p, t: (0, 0)),
                  pl.BlockSpec(perm.shape, lambda p, t: (0, 0)),
                  pl.BlockSpec((2, cin, two_half), pair),
                  pl.BlockSpec((2, two_half, cin), pair),
                  pl.BlockSpec((2, 1, cin), pair),
                  pl.BlockSpec((2, 1, two_half // 2), pair),
                  pl.BlockSpec((2, 1, two_half // 2), pair),
                  pl.BlockSpec((2, nb, two_half), pair)],
        out_specs=[pl.BlockSpec((nb, tc, 2 * cin), lambda p, t: (0, t, p)),
                   pl.BlockSpec((2, nb, two_half), pair)],
        out_shape=[jax.ShapeDtypeStruct((nb, l, d_ssm), BF16),
                   jax.ShapeDtypeStruct((npk, nb, two_half), F32)],
        scratch_shapes=[pltpu.VMEM((rows, 2 * cin), BF16), pltpu.VMEM((rows, 2 * cin), BF16),
                        pltpu.VMEM((2, rows, two_half), F32), pltpu.VMEM((2, rows, two_half), BF16),
                        pltpu.VMEM((2, nb, two_half), F32)],
        compiler_params=_cparams(("arbitrary", "arbitrary"), VMEM_LIMIT),
        name="s5_scan",
    )(u, perm, perm.T, bb, cc, dvec, abar_re, abar_im, s0)


def _s5_pack_weights(bb_re, bb_im, c_re, c_im, d, abar_re, abar_im):
    g, i, p = bb_re.shape
    npk = g // SSM_PACK
    eye = jnp.eye(SSM_PACK, dtype=F32)

    def blockdiag_in(m):
        m = m.reshape(npk, SSM_PACK, i, p)
        return jnp.einsum("ngip,gh->ngihp", m, eye).reshape(npk, SSM_PACK * i, SSM_PACK * p)

    def blockdiag_out(m):
        m = m.reshape(npk, SSM_PACK, i, p)
        return jnp.einsum("ngip,gh->ngphi", m, eye).reshape(npk, SSM_PACK * p, SSM_PACK * i)

    bb = jnp.concatenate([blockdiag_in(bb_re), blockdiag_in(bb_im)], axis=2).astype(BF16)
    cc = jnp.concatenate([blockdiag_out(c_re), -blockdiag_out(c_im)], axis=1).astype(BF16)
    dvec = d.reshape(npk, 1, SSM_PACK * i)
    are = abar_re.reshape(npk, 1, SSM_PACK * p)
    aim = abar_im.reshape(npk, 1, SSM_PACK * p)
    return bb, cc, dvec, are, aim


def _state_to_packs(s_re, s_im):
    b, g, p = s_re.shape
    npk = g // SSM_PACK
    re = jnp.transpose(s_re.reshape(b, npk, SSM_PACK * p), (1, 0, 2))
    im = jnp.transpose(s_im.reshape(b, npk, SSM_PACK * p), (1, 0, 2))
    return jnp.concatenate([re, im], axis=2)


def _packs_to_state(sf, g, p):
    npk, b, two_half = sf.shape
    half = two_half // 2
    re = jnp.transpose(sf[:, :, :half], (1, 0, 2)).reshape(b, g, p)
    im = jnp.transpose(sf[:, :, half:], (1, 0, 2)).reshape(b, g, p)
    return re, im


def _outproj_kernel(attn_ref, ssm_ref, x_ref, mod_ref, wglu_ref, bglu_ref, wout_ref, gpm_ref, gpf_ref,
                    wr_ref, br_ref, *rest, route):
    if route:
        tri_ref, x1_ref, h2_ref, info_ref, pc_ref = rest
    else:
        x1_ref, h2_ref, lg_ref = rest
    y = ssm_ref[0]
    z = jnp.dot(y, wglu_ref[...], preferred_element_type=F32) + bglu_ref[...]
    s = (y.astype(F32) * jax.nn.sigmoid(z)).astype(BF16)
    m = jnp.dot(attn_ref[0], wout_ref[0:D_ATTN, :], preferred_element_type=F32)
    m = m + jnp.dot(s, wout_ref[D_ATTN:, :], preferred_element_type=F32)
    x1 = x_ref[0] + mod_ref[0, 2:3, :] * _rms(m, gpm_ref[...])
    x1_ref[0] = x1
    h2 = (_rms(x1, gpf_ref[...]) * (1.0 + mod_ref[0, 4:5, :]) + mod_ref[0, 3:4, :]).astype(BF16)
    h2_ref[...] = h2
    lg = jnp.dot(h2, wr_ref[...], preferred_element_type=F32) + br_ref[...]
    if route:
        _route_block(lg.T[:N_EXPERTS, :], tri_ref[...], info_ref, pc_ref)
    else:
        lg_ref[...] = lg[:, :N_EXPERTS]


def _outproj(attn, ssm, x, mod, w_glu_bf, b_glu, w_out_bf, g_post_mix, g_pre_ffn, w_router_bf, b_router, *, route):
    b, l, d = x.shape
    tm = min(ROW_TILE, l)
    nt = l // tm
    d_ssm = w_glu_bf.shape[0]
    ne = w_router_bf.shape[1]
    c2 = lambda bi, j: (0, 0)
    flat = lambda bi, j: (bi * nt + j, 0)
    in_specs = [pl.BlockSpec((1, tm, D_ATTN), lambda bi, j: (bi, j, 0)),
                pl.BlockSpec((1, tm, d_ssm), lambda bi, j: (bi, j, 0)),
                pl.BlockSpec((1, tm, d), lambda bi, j: (bi, j, 0)),
                pl.BlockSpec((1, 6, d), lambda bi, j: (bi, 0, 0)),
                pl.BlockSpec(w_glu_bf.shape, c2),
                pl.BlockSpec((1, d_ssm), c2),
                pl.BlockSpec(w_out_bf.shape, c2),
                pl.BlockSpec((1, d), c2),
                pl.BlockSpec((1, d), c2),
                pl.BlockSpec(w_router_bf.shape, c2),
                pl.BlockSpec((1, ne), c2)]
    args = [attn, ssm, x, mod, w_glu_bf, b_glu.reshape(1, d_ssm), w_out_bf, g_post_mix.reshape(1, d),
            g_pre_ffn.reshape(1, d), w_router_bf, b_router.reshape(1, ne)]
    out_specs = [pl.BlockSpec((1, tm, d), lambda bi, j: (bi, j, 0)), pl.BlockSpec((tm, d), flat)]
    out_shape = [jax.ShapeDtypeStruct((b, l, d), F32), jax.ShapeDtypeStruct((b * l, d), BF16)]
    if route:
        assert tm == ROW_TILE
        in_specs.append(pl.BlockSpec((tm, tm), c2))
        args.append(_sort_triangle(tm))
        out_specs += [pl.BlockSpec((2 * TOP_K, tm), lambda bi, j: (0, bi * nt + j)),
                      pl.BlockSpec((1, N_EXPERTS, 1), lambda bi, j: (bi * nt + j, 0, 0))]
        out_shape += [jax.ShapeDtypeStruct((2 * TOP_K, b * l), F32),
                      jax.ShapeDtypeStruct((b * nt, N_EXPERTS, 1), I32)]
    else:
        out_specs.append(pl.BlockSpec((tm, N_EXPERTS), flat))
        out_shape.append(jax.ShapeDtypeStruct((b * l, N_EXPERTS), F32))
    return pl.pallas_call(
        functools.partial(_outproj_kernel, route=route),
        grid=(b, nt),
        in_specs=in_specs,
        out_specs=out_specs,
        out_shape=out_shape,
        compiler_params=_cparams(("arbitrary", "arbitrary"), VMEM_LIMIT),
        name="glu_outproj_norms_router",
    )(*args)


def _route_block(lg, tri, info_ref, pc_ref):
    ne, tb = lg.shape
    eidx = lax.broadcasted_iota(I32, (ne, tb), 0).astype(F32)
    work = lg
    sels, vals = [], []
    for _ in range(TOP_K):
        m = work.max(axis=0, keepdims=True)
        idx = jnp.where(work == m, eidx, float(ne)).min(axis=0, keepdims=True)
        sel = eidx == idx
        sels.append(sel)
        vals.append(m)
        work = jnp.where(sel, -jnp.inf, work)
    ex = [jnp.exp(v - vals[0]) for v in vals]
    den = ex[0] + ex[1] + ex[2] + ex[3]
    onehot = (sels[0] | sels[1] | sels[2] | sels[3]).astype(F32)
    cnt = onehot.sum(axis=1, keepdims=True).astype(I32)
    pc = ((cnt + (BF16_ROWS - 1)) >> 4) << 4
    pc_ref[0] = pc
    rank = jnp.dot(onehot.astype(BF16), tri, preferred_element_type=F32)
    low = jnp.where(lax.broadcasted_iota(I32, (ne, LANES), 1) < lax.broadcasted_iota(I32, (ne, LANES), 0), 1.0, 0.0)
    pcb = jnp.concatenate([jnp.broadcast_to(pc.astype(F32), (ne, LANES)), jnp.zeros((LANES - ne, LANES), F32)], axis=0)
    off = jnp.dot(low.astype(BF16), pcb.astype(BF16), preferred_element_type=F32)[:, 0:1]
    posmat = rank + off
    for k in range(TOP_K):
        info_ref[k:k + 1, :] = jnp.where(sels[k], posmat, 0.0).sum(axis=0, keepdims=True)
        info_ref[TOP_K + k:TOP_K + k + 1, :] = ex[k] / den


def _router_kernel(lg_ref, tri_ref, info_ref, pc_ref):
    _route_block(lg_ref[...], tri_ref[...], info_ref, pc_ref)


def _sort_triangle(tb):
    return jnp.triu(jnp.ones((tb, tb), BF16), k=1)


def _router(logits_t, tb):
    ne, t = logits_t.shape
    nb = t // tb
    tri = _sort_triangle(tb)
    return pl.pallas_call(
        _router_kernel,
        grid=(nb,),
        in_specs=[pl.BlockSpec((ne, tb), lambda i: (0, i)),
                  pl.BlockSpec((tb, tb), lambda i: (0, 0))],
        out_specs=[pl.BlockSpec((2 * TOP_K, tb), lambda i: (0, i)),
                   pl.BlockSpec((1, ne, 1), lambda i: (i, 0, 0))],
        out_shape=[jax.ShapeDtypeStruct((2 * TOP_K, t), F32),
                   jax.ShapeDtypeStruct((nb, ne, 1), I32)],
        compiler_params=_cparams(("arbitrary",)),
        name="router_top4",
    )(logits_t, tri)


def _local_chunks(tb, ne):
    rows = TOP_K * tb + ne * (BF16_ROWS - 1)
    return -(-rows // SORT_ROWS) * (SORT_ROWS // BF16_ROWS)


def _sort_plan(pc, tb, n_tokens):
    nb, ne = pc.shape
    nch = _local_chunks(tb, ne)
    n_tiles = (TOP_K * n_tokens + nb * ne * (BF16_ROWS - 1)) // FFN_TILE + ne
    loc_end = jnp.cumsum(pc, axis=1)
    loc_off = loc_end - pc
    tot = pc.sum(axis=0)
    tot_pad = ((tot + FFN_TILE - 1) // FFN_TILE) * FFN_TILE
    base_end = jnp.cumsum(tot_pad)
    base = base_end - tot_pad
    glob = base[None, :] + jnp.cumsum(pc, axis=0) - pc
    n_used = (base_end[-1] // FFN_TILE).astype(I32)
    tile_row = jnp.arange(n_tiles, dtype=I32) * FFN_TILE
    tile_expert = jnp.minimum((base_end[None, :] <= tile_row[:, None]).sum(axis=1), ne - 1).astype(I32)
    r = jnp.arange(nch, dtype=I32) * BF16_ROWS
    e_of = jnp.minimum((loc_end[:, None, :] <= r[None, :, None]).sum(axis=2), ne - 1)
    shift = glob - loc_off
    owner = e_of[:, :, None] == jnp.arange(ne, dtype=I32)[None, None, :]
    dst = jnp.where(owner, shift[:, None, :], 0).sum(axis=2) + r[None, :]
    n_valid = (loc_end[:, -1] // BF16_ROWS).astype(I32)
    valid = jnp.arange(nch, dtype=I32)[None, :] < n_valid[:, None]
    spare = n_tiles * (FFN_TILE // BF16_ROWS) + (jnp.arange(nb, dtype=I32)[:, None] % 2) * nch \
        + jnp.arange(nch, dtype=I32)[None, :]
    chunk_dst = jnp.where(valid, dst // BF16_ROWS, spare).astype(I32)
    chunk_src = jnp.where(valid, dst // BF16_ROWS, 0).astype(I32)
    tail_n = ((tot_pad - tot) // BF16_ROWS).astype(I32)
    tail_dst = ((base + tot) // BF16_ROWS).astype(I32)
    spare_tiles = -(-2 * nch * BF16_ROWS // FFN_TILE)
    return dict(nch=nch, n_tiles=n_tiles, spare_tiles=spare_tiles, chunk_dst=chunk_dst.reshape(-1),
                chunk_src=chunk_src.reshape(-1), n_used=n_used, tile_expert=tile_expert, tail_n=tail_n,
                tail_dst=tail_dst)


def _chunk_copy(buf, hbm, slot, c, dst_chunk, sem, to_hbm):
    v = buf.at[slot, pl.ds(pl.multiple_of(c * BF16_ROWS, BF16_ROWS), BF16_ROWS)]
    h = hbm.at[pl.ds(pl.multiple_of(dst_chunk * BF16_ROWS, BF16_ROWS), BF16_ROWS)]
    return pltpu.make_async_copy(v, h, sem) if to_hbm else pltpu.make_async_copy(h, v, sem)


def _local_row_ids(rt, tb):
    return lax.broadcasted_iota(I32, (rt, tb), 0).astype(F32).astype(BF16)


def _local_pos(pos_row, r0, rt):
    return jnp.clip(pos_row - float(r0), -1.0, float(rt)).astype(BF16)


def _dispatch_kernel(cd_ref, tn_ref, td_ref, nu_ref, h_ref, info_ref, hx_ref, infox_ref, xs_ref,
                     buf_ref, zero_ref, ztile_ref, sem, zsem, *, nch, rt, n_tiles, spare_tiles):
    b = pl.program_id(0)
    nb = pl.num_programs(0)
    slot = b % 2

    def send(c0, c1):
        for c in range(c0, c1):
            _chunk_copy(buf_ref, xs_ref, slot, c, cd_ref[b * nch + c], sem.at[slot], True).start()

    def sort_block(hr, ir):
        tb = hr.shape[0]
        h = hr[...]
        rid = _local_row_ids(rt, tb)
        one = jnp.ones((rt, tb), BF16)
        nrows = _local_chunks(tb, N_EXPERTS) * BF16_ROWS
        for r0 in range(0, nrows, rt):
            p = jnp.zeros((rt, tb), BF16)
            for k in range(TOP_K):
                p = jnp.where(rid == _local_pos(ir[k:k + 1, :], r0, rt), one, p)
            buf_ref[slot, r0:r0 + rt, :] = jnp.dot(p, h, preferred_element_type=F32).astype(BF16)
            send(r0 // BF16_ROWS, (r0 + rt) // BF16_ROWS)
        send(nrows // BF16_ROWS, nch)

    pl.when(b < nb - 1)(lambda: sort_block(h_ref, info_ref))
    pl.when(b == nb - 1)(lambda: sort_block(hx_ref, infox_ref))

    def drain(step, sl):
        for c in range(nch):
            _chunk_copy(buf_ref, xs_ref, sl, c, cd_ref[step * nch + c], sem.at[sl], True).wait()

    @pl.when(b > 0)
    def _():
        drain(b - 1, 1 - slot)

    @pl.when(b == nb - 1)
    def _():
        drain(b, slot)
        zero_ref[...] = jnp.zeros_like(zero_ref)
        for e in range(N_EXPERTS):
            def zcopy(c, e=e):
                dst = pl.multiple_of((td_ref[e] + c) * BF16_ROWS, BF16_ROWS)
                return pltpu.make_async_copy(zero_ref, xs_ref.at[pl.ds(dst, BF16_ROWS)], zsem)

            lax.fori_loop(0, tn_ref[e], lambda c, _: (zcopy(c).start(), 0)[1], 0)
            lax.fori_loop(0, tn_ref[e], lambda c, _: (zcopy(c).wait(), 0)[1], 0)

        ztile_ref[...] = jnp.zeros_like(ztile_ref)

        def ztile(i):
            dst = pl.multiple_of(i * FFN_TILE, FFN_TILE)
            return pltpu.make_async_copy(ztile_ref, xs_ref.at[pl.ds(dst, FFN_TILE)], zsem)

        lax.fori_loop(nu_ref[0], n_tiles + spare_tiles, lambda i, _: (ztile(i).start(), 0)[1], 0)
        lax.fori_loop(nu_ref[0], n_tiles + spare_tiles, lambda i, _: (ztile(i).wait(), 0)[1], 0)


def _dispatch(h2, info, h2x, infox, plan, tb):
    t, d = h2.shape
    tx = h2x.shape[0]
    nb = t // tb
    nch = plan["nch"]
    rows = (plan["n_tiles"] + plan["spare_tiles"]) * FFN_TILE
    assert (nch * BF16_ROWS) % SORT_ROWS == 0 and tx <= tb
    kern = functools.partial(_dispatch_kernel, nch=nch, rt=SORT_ROWS, n_tiles=plan["n_tiles"],
                             spare_tiles=plan["spare_tiles"])
    grid_spec = pltpu.PrefetchScalarGridSpec(
        num_scalar_prefetch=4,
        grid=(nb + 1,),
        in_specs=[pl.BlockSpec((tb, d), lambda i, *_: (jnp.minimum(i, nb - 1), 0)),
                  pl.BlockSpec((2 * TOP_K, tb), lambda i, *_: (0, jnp.minimum(i, nb - 1))),
                  pl.BlockSpec((tx, d), lambda i, *_: (0, 0)),
                  pl.BlockSpec((2 * TOP_K, tx), lambda i, *_: (0, 0))],
        out_specs=pl.BlockSpec(memory_space=pl.ANY),
        scratch_shapes=[pltpu.VMEM((2, nch * BF16_ROWS, d), BF16),
                        pltpu.VMEM((BF16_ROWS, d), BF16),
                        pltpu.VMEM((FFN_TILE, d), BF16),
                        pltpu.SemaphoreType.DMA((2,)),
                        pltpu.SemaphoreType.DMA],
    )
    return pl.pallas_call(
        kern,
        grid_spec=grid_spec,
        out_shape=jax.ShapeDtypeStruct((rows, d), BF16),
        compiler_params=_cparams(("arbitrary",), VMEM_LIMIT),
        name="moe_dispatch",
    )(plan["chunk_dst"], plan["tail_n"], plan["tail_dst"], plan["n_used"].reshape(1), h2, info, h2x, infox)


def _ffn_kernel(te_ref, nu_ref, x_ref, wgu_ref, bgu_ref, wd_ref, bd_ref, perm_ref, y_ref, wgu_s, wd_s):
    i = pl.program_id(0)
    f = wd_s.shape[0]
    prev = te_ref[jnp.maximum(i - 1, 0)]
    new_expert = (i == 0) | (te_ref[i] != prev)

    @pl.when((i < nu_ref[0]) & new_expert)
    def _():
        for c in range(0, 2 * f, 2 * LANES):
            blk = wgu_ref[0, :, c:c + 2 * LANES].astype(BF16)
            wgu_s[:, c:c + 2 * LANES] = jnp.dot(blk, perm_ref[...], preferred_element_type=F32).astype(BF16)
        wd_s[...] = wd_ref[0].astype(BF16)

    @pl.when(i < nu_ref[0])
    def _():
        x = x_ref[...]
        acts = []
        for c in range(0, 2 * f, 2 * LANES):
            gu = jnp.dot(x, wgu_s[:, c:c + 2 * LANES], preferred_element_type=F32) + bgu_ref[0, :, c:c + 2 * LANES]
            glu = jnp.minimum(gu[:, :LANES], SWIGLU_LIMIT)
            lin = jnp.clip(gu[:, LANES:], -SWIGLU_LIMIT, SWIGLU_LIMIT)
            acts.append((glu * jax.nn.sigmoid(SWIGLU_ALPHA * glu) * (lin + 1.0)).astype(BF16))
        act = jnp.concatenate(acts, axis=1)
        y_ref[...] = (jnp.dot(act, wd_s[...], preferred_element_type=F32) + bd_ref[0]).astype(BF16)


def _ffn(xs, plan, w_gate_up, b_gu_perm, w_down, b_down):
    rows, d = xs.shape
    ne, _, f2 = w_gate_up.shape
    f = f2 // 2
    n_tiles = plan["n_tiles"]
    src = jnp.concatenate([jnp.arange(LANES) * 2, jnp.arange(LANES) * 2 + 1])
    perm = (jnp.arange(2 * LANES)[:, None] == src[None, :]).astype(BF16)

    def xi(i, te, nu):
        return (jnp.minimum(i, nu[0] - 1), 0)

    def wi(i, te, nu):
        return (te[jnp.minimum(i, nu[0] - 1)], 0, 0)

    grid_spec = pltpu.PrefetchScalarGridSpec(
        num_scalar_prefetch=2,
        grid=(n_tiles,),
        in_specs=[pl.BlockSpec((FFN_TILE, d), xi),
                  pl.BlockSpec((1, d, f2), wi),
                  pl.BlockSpec((1, 1, f2), wi),
                  pl.BlockSpec((1, f, d), wi),
                  pl.BlockSpec((1, 1, d), wi),
                  pl.BlockSpec((2 * LANES, 2 * LANES), lambda i, te, nu: (0, 0))],
        out_specs=pl.BlockSpec((FFN_TILE, d), xi),
        scratch_shapes=[pltpu.VMEM((d, f2), BF16), pltpu.VMEM((f, d), BF16)],
    )
    return pl.pallas_call(
        _ffn_kernel,
        grid_spec=grid_spec,
        out_shape=jax.ShapeDtypeStruct((rows, d), BF16),
        input_output_aliases={2: 0},
        compiler_params=_cparams(("arbitrary",), VMEM_LIMIT),
        name="moe_expert_ffn",
    )(plan["tile_expert"], plan["n_used"].reshape(1), xs, w_gate_up, b_gu_perm.reshape(ne, 1, f2), w_down,
      b_down.reshape(ne, 1, d), perm)


def _combine_kernel(cd_ref, ys_ref, info_ref, x1_ref, mod_ref, infox_ref, x1x_ref, modx_ref, g_ref,
                    o_ref, ox_ref, buf_ref, w_ref, sem, *, nch, rt):
    b = pl.program_id(0)
    nb = pl.num_programs(0)
    slot = b % 2

    def fetch(step, sl, c0, c1):
        for c in range(c0, c1):
            _chunk_copy(buf_ref, ys_ref, sl, c, cd_ref[step * nch + c], sem.at[sl], False).start()

    @pl.when(b == 0)
    def _():
        fetch(0, 0, 0, nch)

    def unsort_block(ir, x1r, modr, outr, prefetch_next):
        tb = x1r.shape[0]
        nbat = modr.shape[0]
        rb = tb // nbat
        nrows = _local_chunks(tb, N_EXPERTS) * BF16_ROWS
        per_tile = nch // (nrows // rt)
        rid = _local_row_ids(rt, tb)
        for j, r0 in enumerate(range(0, nrows, rt)):
            w = jnp.zeros((rt, tb), BF16)
            for k in range(TOP_K):
                gate = jnp.broadcast_to(ir[TOP_K + k:TOP_K + k + 1, :].astype(BF16), (rt, tb))
                w = jnp.where(rid == _local_pos(ir[k:k + 1, :], r0, rt), gate, w)
            w_ref[r0:r0 + rt, 0:tb] = w
            if prefetch_next:
                fetch(b + 1, 1 - slot, j * per_tile, (j + 1) * per_tile)

        for c in range(nch):
            _chunk_copy(buf_ref, ys_ref, slot, c, cd_ref[b * nch + c], sem.at[slot], False).wait()

        moe = lax.dot_general(w_ref[0:nrows, 0:tb], buf_ref[slot, 0:nrows, :], (((0,), (0,)), ((), ())),
                              preferred_element_type=F32)
        out = _rms(moe, g_ref[...])
        for s in range(nbat):
            rs = slice(s * rb, (s + 1) * rb)
            outr[rs, :] = x1r[rs, :] + modr[s, 5:6, :] * out[rs, :]

    pl.when(b < nb - 1)(lambda: unsort_block(info_ref, x1_ref, mod_ref, o_ref, True))
    pl.when(b == nb - 1)(lambda: unsort_block(infox_ref, x1x_ref, modx_ref, ox_ref, False))


def _combine(ys, info, x1, mod, infox, x1x, modx, g_post_ffn, plan, tb, l):
    t, d = x1.shape
    tx = x1x.shape[0]
    nb = t // tb
    nch = plan["nch"]
    assert l % tb == 0
    clamp = lambda i: jnp.minimum(i, nb - 1)
    kern = functools.partial(_combine_kernel, nch=nch, rt=SORT_ROWS)
    grid_spec = pltpu.PrefetchScalarGridSpec(
        num_scalar_prefetch=1,
        grid=(nb + 1,),
        in_specs=[pl.BlockSpec(memory_space=pl.ANY),
                  pl.BlockSpec((2 * TOP_K, tb), lambda i, *_: (0, clamp(i))),
                  pl.BlockSpec((tb, d), lambda i, *_: (clamp(i), 0)),
                  pl.BlockSpec((1, 6, d), lambda i, *_: (clamp(i) // (l // tb), 0, 0)),
                  pl.BlockSpec((2 * TOP_K, tx), lambda i, *_: (0, 0)),
                  pl.BlockSpec((tx, d), lambda i, *_: (0, 0)),
                  pl.BlockSpec(modx.shape, lambda i, *_: (0, 0, 0)),
                  pl.BlockSpec((1, d), lambda i, *_: (0, 0))],
        out_specs=[pl.BlockSpec((tb, d), lambda i, *_: (clamp(i), 0)),
                   pl.BlockSpec((tx, d), lambda i, *_: (0, 0))],
        scratch_shapes=[pltpu.VMEM((2, nch * BF16_ROWS, d), BF16), pltpu.VMEM((nch * BF16_ROWS, tb), BF16),
                        pltpu.SemaphoreType.DMA((2,))],
    )
    return pl.pallas_call(
        kern,
        grid_spec=grid_spec,
        out_shape=[jax.ShapeDtypeStruct((t, d), F32), jax.ShapeDtypeStruct((tx, d), F32)],
        compiler_params=_cparams(("arbitrary",), VMEM_LIMIT),
        name="moe_combine",
    )(plan["chunk_src"], ys, info, x1, mod, infox, x1x, modx, g_post_ffn.reshape(1, d))


def _moe(main, second, l, g_post_ffn, w_gate_up, b_gu_perm, w_down, b_down):
    h2, info, pc, x1, mod = main
    h2x, infox, pcx, x1x, modx = second
    t = h2.shape[0]
    tx = h2x.shape[0]
    tb = ROW_TILE
    pc_all = jnp.concatenate([pc.reshape(t // tb, N_EXPERTS), pcx.reshape(1, N_EXPERTS)], axis=0)
    plan = _sort_plan(pc_all, tb, t + tx)
    xs = _dispatch(h2, info, h2x, infox, plan, tb)
    ys = _ffn(xs, plan, w_gate_up, b_gu_perm, w_down, b_down)
    return _combine(ys, info, x1, mod, infox, x1x, modx, g_post_ffn, plan, tb, l)


def _stream(x, mod, wts, attn_fn, s0_packs):
    b, l, d = x.shape
    q, k_att, v_att, u, kf, vf = _inproj(x, mod, wts["g_pre_mix"], wts["w_in"], pad_front=attn_fn is None)
    if attn_fn is None:
        attn = _attn_prompt(q, k_att, v_att, wts["table_prompt"])
    else:
        attn = attn_fn(q, k_att, v_att)
    ssm, sf = _s5_scan(u, wts["bb"], wts["cc"], wts["dvec"], wts["are"], wts["aim"], s0_packs)
    route = l >= ROW_TILE
    x1, h2, *routing = _outproj(attn, ssm, x, mod, wts["w_glu"], wts["b_glu"], wts["w_out"], wts["g_post_mix"],
                                wts["g_pre_ffn"], wts["w_router"], wts["b_router"], route=route)
    if not route:
        routing = _router(routing[0].T, b * l)
    info, pc = routing
    return (h2, info, pc, x1.reshape(b * l, d), mod), kf, vf, sf


def kernel(x_prompt, x_sample, cache_k, cache_v, state_ssm_re, state_ssm_im, c_prompt, c_sample, w_ada, b_ada, g_pre_mix, g_post_mix, g_pre_ffn, g_post_ffn, w_in, rel_bias, ssm_a_re, ssm_a_im, ssm_log_dt, ssm_b_re, ssm_b_im, ssm_c_re, ssm_c_im, ssm_d, w_glu, b_glu, w_out, w_router, b_router, w_gate_up, b_gate_up, w_down, b_down):
    depth = w_ada.shape[0]
    assert depth == 1, "single-layer trunk"
    bp, lp, d = x_prompt.shape
    bs, ls, _ = x_sample.shape
    n_cache = cache_k.shape[2]
    past_len = PAST_LEN
    g, p = ssm_a_re.shape[1:]
    lyr = 0

    mod = _ada(jnp.concatenate([c_prompt, c_sample], axis=0), w_ada[lyr], b_ada[lyr])
    mod = mod.reshape(bp + bs, 6, d)

    abre, abim, bbre, bbim = _s5_prep(ssm_a_re[lyr], ssm_a_im[lyr], ssm_log_dt[lyr], ssm_b_re[lyr], ssm_b_im[lyr])
    bb, cc, dvec, are, aim = _s5_pack_weights(bbre, bbim, ssm_c_re[lyr], ssm_c_im[lyr], ssm_d[lyr], abre, abim)

    f2 = w_gate_up.shape[-1]
    b_gu_perm = b_gate_up[lyr].reshape(N_EXPERTS, f2 // (2 * LANES), LANES, 2)
    b_gu_perm = jnp.transpose(b_gu_perm, (0, 1, 3, 2)).reshape(N_EXPERTS, f2)

    rb = rel_bias[lyr]
    wts = dict(
        g_pre_mix=g_pre_mix[lyr], g_post_mix=g_post_mix[lyr], g_pre_ffn=g_pre_ffn[lyr], g_post_ffn=g_post_ffn[lyr],
        w_in=w_in[lyr].astype(BF16), w_glu=w_glu[lyr].astype(BF16), b_glu=b_glu[lyr], w_out=w_out[lyr].astype(BF16),
        w_router=jnp.pad(w_router[lyr], ((0, 0), (0, LANES - N_EXPERTS))).astype(BF16),
        b_router=jnp.pad(b_router[lyr], (0, LANES - N_EXPERTS)),
        w_gate_up=w_gate_up[lyr], b_gu_perm=b_gu_perm, w_down=w_down[lyr], b_down=b_down[lyr],
        bb=bb, cc=cc, dvec=dvec, are=are, aim=aim,
        table_prompt=_bias_table(rb, q0=WIN_ROWS, k0=0, nq=CHUNK, nk=Q_TILE + WIN_ROWS, strips=Q_TILE // CHUNK),
    )

    zero_state = jnp.zeros((g // SSM_PACK, bp, 2 * SSM_PACK * p), F32)
    ffn_in_p, kf_p, vf_p, sf_p = _stream(x_prompt, mod[:bp], wts, None, zero_state)

    tab_s = _bias_table(rb, q0=past_len, k0=past_len - n_cache, nq=ls, nk=n_cache + ls)
    ck = cache_k[lyr].reshape(bs, n_cache, D_ATTN)
    cv = cache_v[lyr].reshape(bs, n_cache, D_ATTN)
    attn_s = lambda q, k, v: _attn_sample(q, k, v, ck, cv, tab_s)
    s0 = _state_to_packs(state_ssm_re[lyr], state_ssm_im[lyr])
    ffn_in_s, kf_s, vf_s, sf_s = _stream(x_sample, mod[bp:], wts, attn_s, s0)

    y_p, y_s = _moe(ffn_in_p, ffn_in_s, lp, wts["g_post_ffn"], wts["w_gate_up"], wts["b_gu_perm"], wts["w_down"],
                    wts["b_down"])
    y_p = y_p.reshape(bp, lp, d)
    y_s = y_s.reshape(bs, ls, d)

    def heads(a):
        return a.reshape(1, a.shape[0], a.shape[1], N_HEADS, HEAD_DIM)

    sre_p, sim_p = _packs_to_state(sf_p, g, p)
    sre_s, sim_s = _packs_to_state(sf_s, g, p)
    return (y_p, y_s, heads(kf_p), heads(vf_p), sre_p[None], sim_p[None],
            heads(kf_s), heads(vf_s), sre_s[None], sim_s[None])
```
